```python
import jax, jax.numpy as jnp
from jax import lax
import numpy as np

D_MODEL = 2048
BATCH = 1
SEQ = 8192
DEPTH = 1

N_HEADS_ATTN = 8
HEAD_DIM = 128
N_IDX_HEADS = 16
IDX_DIM = 64
TOPK_MAX = 256
Q_BLOCK = 128
N_RET_HEADS = 8
RET_QK_DIM = 128
RET_V_DIM = 256
RET_CHUNK = 128
N_MEM = 256
N_MEM_HEADS = 4
MEM_HEAD_DIM = 256
D_FF = -(-8 * D_MODEL // (3 * 256)) * 256
ROPE_THETA = 10000.0
EPS = 1e-6
N_BRANCHES = 3

A_WIDTH = N_HEADS_ATTN * HEAD_DIM
IDX_Q_WIDTH = N_IDX_HEADS * IDX_DIM
RET_QK_WIDTH = N_RET_HEADS * RET_QK_DIM
RET_V_WIDTH = N_RET_HEADS * RET_V_DIM
MEM_WIDTH = N_MEM_HEADS * MEM_HEAD_DIM
COL_SIZES = (A_WIDTH, A_WIDTH, A_WIDTH, IDX_Q_WIDTH, IDX_DIM, N_IDX_HEADS,
             RET_QK_WIDTH, RET_QK_WIDTH, RET_V_WIDTH, RET_V_WIDTH, MEM_WIDTH, N_BRANCHES * D_MODEL)
COL_SPLITS = tuple(sum(COL_SIZES[:i + 1]) for i in range(len(COL_SIZES) - 1))
IN_COLS = sum(COL_SIZES)

kernel_name = "hybrid_dsa_retention_memory_gated_block"


def rms_norm(x, g):
    xf = x.astype(jnp.float32)
    y = xf * lax.rsqrt(jnp.mean(xf * xf, axis=-1, keepdims=True) + EPS)
    return (y * g.astype(jnp.float32)).astype(x.dtype)


def rope(x, pos):
    d = x.shape[-1]
    freqs = ROPE_THETA ** (-jnp.arange(0, d, 2, dtype=jnp.float32) / d)
    ang = pos.astype(jnp.float32)[..., None] * freqs
    cos = jnp.cos(ang)[:, :, None, :]
    sin = jnp.sin(ang)[:, :, None, :]
    xf = x.astype(jnp.float32)
    x1, x2 = xf[..., : d // 2], xf[..., d // 2:]
    return jnp.concatenate([x1 * cos - x2 * sin, x2 * cos + x1 * sin], axis=-1).astype(x.dtype)


def dsa_attention(q, k, v, q_idx, k_idx, w_idx):
    B, S, H, dh = q.shape
    k_top = min(TOPK_MAX, S // 4)
    nb = S // Q_BLOCK

    def to_blocks(a):
        return jnp.moveaxis(a.reshape(B, nb, Q_BLOCK, *a.shape[2:]), 1, 0)

    t_blk = jnp.arange(S, dtype=jnp.int32).reshape(nb, Q_BLOCK)
    key_pos = jnp.arange(S, dtype=jnp.int32)
    gather = jax.vmap(lambda a, i: a[i])

    def block(args):
        qb, qib, wb, tb = args
        rel = jax.nn.relu(jnp.einsum('bqhd,bsd->bqhs', qib, k_idx))
        score = jnp.einsum('bqhs,bqh->bqs', rel, wb).astype(jnp.float32)
        causal = key_pos[None, None, :] <= tb[None, :, None]
        score = jnp.where(causal, score, -jnp.inf)
        _, idx = lax.top_k(score, k_top)
        valid = idx <= tb[None, :, None]
        ks = gather(k, idx)
        vs = gather(v, idx)
        logits = jnp.einsum('bqhd,bqkhd->bqhk', qb, ks).astype(jnp.float32)
        logits = jnp.where(valid[:, :, None, :], logits, -jnp.inf)
        p = jax.nn.softmax(logits, axis=-1).astype(v.dtype)
        return jnp.einsum('bqhk,bqkhd->bqhd', p, vs)

    out = lax.map(block, (to_blocks(q), to_blocks(q_idx), to_blocks(w_idx), t_blk))
    return jnp.moveaxis(out, 0, 1).reshape(B, S, H * dh)


def retention(q, k, v, g):
    B, S, H, dk = q.shape
    dv = v.shape[-1]
    C = RET_CHUNK
    n = S // C
    dt = v.dtype
    log_gamma = jnp.log1p(-jnp.exp2(-5.0 - jnp.arange(H, dtype=jnp.float32)))
    i = jnp.arange(C, dtype=jnp.float32)
    diff = i[:, None] - i[None, :]
    inner_decay = jnp.where(diff[None] >= 0,
                            jnp.exp(jnp.maximum(diff, 0.0)[None] * log_gamma[:, None, None]), 0.0)
    k_decay = jnp.exp((C - 1 - i)[None, :] * log_gamma[:, None])
    q_decay = jnp.exp((i + 1)[None, :] * log_gamma[:, None])
    chunk_decay = jnp.exp(C * log_gamma)

    qc = q.reshape(B, n, C, H, dk)
    kc = (k * dk ** -0.5).reshape(B, n, C, H, dk)
    vc = v.reshape(B, n, C, H, dv)

    scores = jnp.einsum('bnihd,bnjhd->bnhij', qc, kc) * inner_decay.astype(dt)
    inner = jnp.einsum('bnhij,bnjhe->bnihe', scores, vc)
    kv = jnp.einsum('bnjhd,bnjhe,hj->nbhde', kc, vc, k_decay.astype(dt)).astype(jnp.float32)

    def step(state, kv_n):
        return state * chunk_decay[None, :, None, None] + kv_n, state

    _, prev = lax.scan(step, jnp.zeros((B, H, dk, dv), jnp.float32), kv)
    cross = jnp.einsum('bnihd,nbhde->bnihe', qc, prev.astype(dt)) * \
        jnp.transpose(q_decay)[None, None, :, :, None].astype(dt)
    y = (inner + cross).reshape(B, S, H, dv).astype(jnp.float32)
    mu = jnp.mean(y, axis=-1, keepdims=True)
    var = jnp.mean(jnp.square(y - mu), axis=-1, keepdims=True)
    yn = ((y - mu) * lax.rsqrt(var + EPS)).astype(dt).reshape(B, S, H * dv)
    return jax.nn.silu(g) * yn


def memory_attention(q, k, v):
    B, S, H, d = q.shape
    logits = jnp.einsum('bshd,bmhd->bhsm', q, k).astype(jnp.float32)
    p = jax.nn.softmax(logits, axis=-1).astype(v.dtype)
    return jnp.einsum('bhsm,bmhd->bshd', p, v).reshape(B, S, H * d)


def setup_inputs(seed: int = 0) -> dict:
    key = jax.random.key(seed)
    ks = jax.random.split(key, 16)

    def w(k, fan_in, fan_out):
        return jax.random.normal(k, (DEPTH, fan_in, fan_out), jnp.float32) * fan_in ** -0.5

    def gain(k):
        return 1.0 + 0.05 * jax.random.normal(k, (DEPTH, D_MODEL), jnp.float32)

    return {
        "x": jax.random.normal(ks[0], (BATCH, SEQ, D_MODEL), jnp.float32),
        "mem": jax.random.normal(ks[1], (BATCH, N_MEM, D_MODEL), jnp.float32),
        "positions": jnp.broadcast_to(jnp.arange(SEQ, dtype=jnp.int32), (BATCH, SEQ)),
        "g_pre_mix": gain(ks[2]),
        "g_mem": gain(ks[3]),
        "w_in": w(ks[4], D_MODEL, IN_COLS),
        "w_mem_kv": w(ks[5], D_MODEL, 2 * MEM_WIDTH),
        "w_branch_a": w(ks[6], A_WIDTH, D_MODEL),
        "w_branch_b": w(ks[7], RET_V_WIDTH, D_MODEL),
        "w_branch_c": w(ks[8], MEM_WIDTH, D_MODEL),
        "w_out": w(ks[9], D_MODEL, D_MODEL),
        "g_post_mix": gain(ks[10]),
        "g_pre_ffn": gain(ks[11]),
        "w_ffn_in": w(ks[12], D_MODEL, 2 * D_FF),
        "w_ffn_out": w(ks[13], D_FF, D_MODEL),
        "g_post_ffn": gain(ks[14]),
    }


def reference(x, mem, positions, g_pre_mix, g_mem, w_in, w_mem_kv, w_branch_a, w_branch_b,
              w_branch_c, w_out, g_post_mix, g_pre_ffn, w_ffn_in, w_ffn_out, g_post_ffn):
    B, S, D = x.shape
    M = mem.shape[1]
    for l in range(DEPTH):
        h = rms_norm(x, g_pre_mix[l])
        proj = h @ w_in[l]
        (a_q, a_k, a_v, i_q, i_k, i_w, r_q, r_k, r_v, r_g, m_q, gate_logits) = \
            jnp.split(proj, COL_SPLITS, axis=-1)

        a_q = rope(a_q.reshape(B, S, N_HEADS_ATTN, HEAD_DIM), positions) * HEAD_DIM ** -0.5
        a_k = rope(a_k.reshape(B, S, N_HEADS_ATTN, HEAD_DIM), positions)
        a_v = a_v.reshape(B, S, N_HEADS_ATTN, HEAD_DIM)
        i_q = rope(i_q.reshape(B, S, N_IDX_HEADS, IDX_DIM), positions) * IDX_DIM ** -0.5
        i_k = rope(i_k.reshape(B, S, 1, IDX_DIM), positions)[:, :, 0, :]
        i_w = i_w * N_IDX_HEADS ** -0.5
        o_a = dsa_attention(a_q, a_k, a_v, i_q, i_k, i_w)

        r_q = rope(r_q.reshape(B, S, N_RET_HEADS, RET_QK_DIM), positions)
        r_k = rope(r_k.reshape(B, S, N_RET_HEADS, RET_QK_DIM), positions)
        r_v = r_v.reshape(B, S, N_RET_HEADS, RET_V_DIM)
        o_b = retention(r_q, r_k, r_v, r_g)

        mem_kv = rms_norm(mem, g_mem[l]) @ w_mem_kv[l]
        m_k, m_v = jnp.split(mem_kv, 2, axis=-1)
        m_k = m_k.reshape(B, M, N_MEM_HEADS, MEM_HEAD_DIM)
        m_v = m_v.reshape(B, M, N_MEM_HEADS, MEM_HEAD_DIM)
        m_q = m_q.reshape(B, S, N_MEM_HEADS, MEM_HEAD_DIM) * MEM_HEAD_DIM ** -0.5
        o_c = memory_attention(m_q, m_k, m_v)

        gates = jax.nn.sigmoid(gate_logits).reshape(B, S, N_BRANCHES, D)
        mixed = (gates[:, :, 0] * (o_a @ w_branch_a[l])
                 + gates[:, :, 1] * (o_b @ w_branch_b[l])
                 + gates[:, :, 2] * (o_c @ w_branch_c[l]))
        x = x + rms_norm(mixed @ w_out[l], g_post_mix[l])

        h2 = rms_norm(x, g_pre_ffn[l])
        f_gate, f_up = jnp.split(h2 @ w_ffn_in[l], 2, axis=-1)
        y2 = (jax.nn.silu(f_gate) * f_up) @ w_ffn_out[l]
        x = x + rms_norm(y2, g_post_ffn[l])
    return x
```

```python
import functools

import jax
import jax.numpy as jnp
import numpy as np
from jax import lax
from jax.experimental import pallas as pl
from jax.experimental.pallas import tpu as pltpu

D_MODEL = 2048
N_HEADS_ATTN = 8
HEAD_DIM = 128
N_IDX_HEADS = 16
IDX_DIM = 64
TOPK_MAX = 256
N_RET_HEADS = 8
RET_QK_DIM = 128
RET_V_DIM = 256
RET_CHUNK = 128
N_MEM_HEADS = 4
MEM_HEAD_DIM = 256
ROPE_THETA = 10000.0
EPS = 1e-6

A_WIDTH = N_HEADS_ATTN * HEAD_DIM
IDX_Q_WIDTH = N_IDX_HEADS * IDX_DIM
RET_QK_WIDTH = N_RET_HEADS * RET_QK_DIM
RET_V_WIDTH = N_RET_HEADS * RET_V_DIM
MEM_WIDTH = N_MEM_HEADS * MEM_HEAD_DIM

LANES = 128
SUBLANES = 8
VMEM_BYTES_V7X = 64 * 1024 * 1024
INT_MIN = -(2 ** 31)
NEG_BIG = -1e30

F32 = jnp.float32
BF16 = jnp.bfloat16
I32 = jnp.int32

PLAIN, ROPE128, ROPE64, SIGMOID = 0, 1, 2, 3


def _cparams(sem, vmem_mb):
    assert vmem_mb * 1024 * 1024 < VMEM_BYTES_V7X
    return pltpu.CompilerParams(dimension_semantics=sem, vmem_limit_bytes=vmem_mb * 1024 * 1024)


def _rms(x, g):
    return x * lax.rsqrt(jnp.mean(x * x, axis=-1, keepdims=True) + EPS) * g


def _rmsnorm_kernel(x_ref, g_ref, o_ref):
    o_ref[...] = _rms(x_ref[...], g_ref[...]).astype(o_ref.dtype)


def _rmsnorm(x, g, tm):
    n, d = x.shape
    return pl.pallas_call(
        _rmsnorm_kernel,
        grid=(n // tm,),
        in_specs=[pl.BlockSpec((tm, d), lambda i: (i, 0)), pl.BlockSpec((1, d), lambda i: (0, 0))],
        out_specs=pl.BlockSpec((tm, d), lambda i: (i, 0)),
        out_shape=jax.ShapeDtypeStruct((n, d), BF16),
        compiler_params=_cparams(("parallel",), 24),
        name="rmsnorm",
    )(x, g.reshape(1, d))


def _rope_table_kernel(pos_ref, c_ref, c128_ref, s128_ref, c64_ref, sa64_ref, sb64_ref,
                       ck_ref, sak_ref, sbk_ref):
    pos = pos_ref[...]
    ang128 = pos * c_ref[0:1, :]
    ang64 = pos * c_ref[1:2, :]
    c128_ref[...] = jnp.cos(ang128)
    s128_ref[...] = jnp.sin(ang128) * c_ref[2:3, :]
    cos64 = jnp.cos(ang64)
    sin64 = jnp.sin(ang64)
    sa = sin64 * c_ref[3:4, :]
    sb = sin64 * c_ref[4:5, :]
    c64_ref[...] = cos64
    sa64_ref[...] = sa
    sb64_ref[...] = sb
    ck_ref[...] = cos64 * c_ref[5:6, :] + c_ref[6:7, :]
    sak_ref[...] = sa * c_ref[5:6, :]
    sbk_ref[...] = sb * c_ref[5:6, :]


def _rope_tables(positions, tm):
    s = positions.shape[-1]
    pos_b = jnp.broadcast_to(positions.reshape(s, 1).astype(F32), (s, LANES))
    lane = np.arange(LANES)
    f128 = ROPE_THETA ** (-jnp.arange(0, HEAD_DIM, 2, dtype=F32) / HEAD_DIM)
    f64 = ROPE_THETA ** (-jnp.arange(0, IDX_DIM, 2, dtype=F32) / IDX_DIM)
    half64 = (lane % IDX_DIM) < IDX_DIM // 2
    rows = [
        jnp.tile(f128, 2),
        jnp.tile(f64, 4),
        jnp.asarray(np.where(lane < HEAD_DIM // 2, -1.0, 1.0), F32),
        jnp.asarray(np.where(half64, -1.0, 0.0), F32),
        jnp.asarray(np.where(half64, 0.0, 1.0), F32),
        jnp.asarray(np.where(lane < IDX_DIM, 1.0, 0.0), F32),
        jnp.asarray(np.where(lane < IDX_DIM, 0.0,
                             np.where(lane < IDX_DIM + N_IDX_HEADS, N_IDX_HEADS ** -0.5, 1.0)), F32),
        jnp.zeros((LANES,), F32),
    ]
    consts = jnp.stack(rows)
    spec = pl.BlockSpec((tm, LANES), lambda i: (i, 0))
    return pl.pallas_call(
        _rope_table_kernel,
        grid=(s // tm,),
        in_specs=[spec, pl.BlockSpec((SUBLANES, LANES), lambda i: (0, 0))],
        out_specs=[spec] * 8,
        out_shape=[jax.ShapeDtypeStruct((s, LANES), F32)] * 8,
        compiler_params=_cparams(("parallel",), 24),
        name="rope_tables",
    )(pos_b, consts)


def _rope128(x, cos, sin_signed):
    return x * cos + pltpu.roll(x, HEAD_DIM // 2, 1) * sin_signed


def _rope64(x, cos, sin_a, sin_b):
    return (x * cos + pltpu.roll(x, LANES - IDX_DIM // 2, 1) * sin_a
            + pltpu.roll(x, IDX_DIM // 2, 1) * sin_b)


def _proj_kernel(a_ref, b_ref, t0_ref, t1_ref, t2_ref, o_ref, *, groups, tn):
    j = pl.program_id(1)
    acc = jnp.dot(a_ref[...], b_ref[...], preferred_element_type=F32)
    for (mode, scale), tiles in groups.items():
        cond = functools.reduce(jnp.logical_or, [j == t for t in tiles])

        @pl.when(cond)
        def _(mode=mode, scale=scale):
            if mode == PLAIN:
                o_ref[...] = (acc * scale).astype(o_ref.dtype)
            elif mode == SIGMOID:
                o_ref[...] = jax.nn.sigmoid(acc).astype(o_ref.dtype)
            else:
                for c in range(tn // LANES):
                    x = acc[:, c * LANES:(c + 1) * LANES]
                    if mode == ROPE128:
                        y = _rope128(x, t0_ref[...], t1_ref[...])
                    else:
                        y = _rope64(x, t0_ref[...], t1_ref[...], t2_ref[...])
                    o_ref[:, c * LANES:(c + 1) * LANES] = (y * scale).astype(o_ref.dtype)


def _proj(a, b, tables, tile_modes, out_dtype, tm, tn, vmem_mb, name):
    m, k = a.shape
    n = b.shape[1]
    assert n // tn == len(tile_modes)
    groups = {}
    for t, ms in enumerate(tile_modes):
        groups.setdefault(ms, []).append(t)
    tspec = pl.BlockSpec((tm, LANES), lambda i, j: (i, 0))
    return pl.pallas_call(
        functools.partial(_proj_kernel, groups=groups, tn=tn),
        grid=(m // tm, n // tn),
        in_specs=[pl.BlockSpec((tm, k), lambda i, j: (i, 0)),
                  pl.BlockSpec((k, tn), lambda i, j: (0, j)), tspec, tspec, tspec],
        out_specs=pl.BlockSpec((tm, tn), lambda i, j: (i, j)),
        out_shape=jax.ShapeDtypeStruct((m, n), out_dtype),
        compiler_params=_cparams(("parallel", "arbitrary"), vmem_mb),
        name=name,
    )(a, b, *tables)


IDX_TQ = 256
IDX_KT = 512
IDX_RC = 64
IDX_BITS = 13


def _to_key(score):
    score = jnp.where(score == 0.0, 0.0, score)
    bits = lax.bitcast_convert_type(score, I32)
    return bits ^ (lax.shift_right_arithmetic(bits, 31) & jnp.int32(0x7FFFFFFF))


def _indexer_kernel(ikw_ref, iq_ref, mask_ref, keys_ref, *, seq, k_top):
    b = pl.program_id(0)
    tq, kt, rc = IDX_TQ, IDX_KT, IDX_RC
    q0 = b * tq
    ntile = q0 // kt + 1
    nrows = ntile * kt
    nchunk = nrows // rc

    w_t = jnp.transpose(ikw_ref[pl.ds(pl.multiple_of(q0, tq), tq), :])
    qidx = q0 + lax.broadcasted_iota(I32, (kt, tq), 1)

    def score_tile(t, carry):
        r0 = pl.multiple_of(t * kt, kt)
        ik = ikw_ref[pl.ds(r0, kt), :].astype(BF16)
        acc = jnp.zeros((kt, tq), F32)
        for h in range(N_IDX_HEADS):
            qh = iq_ref[:, h * LANES:(h + 1) * LANES]
            s = lax.dot_general(ik, qh, (((1,), (1,)), ((), ())), preferred_element_type=F32)
            acc = acc + jnp.maximum(s, 0.0) * w_t[IDX_DIM + h:IDX_DIM + h + 1, :]
        kidx = r0 + lax.broadcasted_iota(I32, (kt, tq), 0)
        keys_ref[pl.ds(r0, kt), :] = jnp.where(kidx <= qidx, _to_key(acc), INT_MIN)
        return carry

    lax.fori_loop(0, ntile, score_tile, 0)

    def count(pred):
        def body(c, cnt):
            r0 = pl.multiple_of(c * rc, rc)
            blk = keys_ref[pl.ds(r0, rc), :]
            idx = r0 + lax.broadcasted_iota(I32, (rc, tq), 0)
            hit = pred(blk, idx)
            return cnt + jnp.sum(hit.reshape(rc // SUBLANES, SUBLANES, tq), axis=0)
        cnt = lax.fori_loop(0, nchunk, body, jnp.zeros((SUBLANES, tq), I32))
        return jnp.sum(cnt, axis=0, keepdims=True)

    def value_pass(p, thr_u):
        cand = thr_u | lax.shift_left(jnp.int32(1), 31 - p)
        cand_s = cand ^ INT_MIN
        cnt = count(lambda blk, idx: jnp.where(blk >= cand_s, 1, 0))
        return jnp.where(cnt >= k_top, cand, thr_u)

    thr = lax.fori_loop(0, 32, value_pass, jnp.zeros((1, tq), I32)) ^ INT_MIN
    cnt_gt = count(lambda blk, idx: jnp.where(blk > thr, 1, 0))
    cnt_eq = count(lambda blk, idx: jnp.where(blk == thr, 1, 0))
    need = k_top - cnt_gt
    tied = jnp.where(thr != INT_MIN, jnp.where(cnt_eq > need, 1, 0), 0)

    def tie_break():
        def index_pass(p, x):
            cand = x | lax.shift_left(jnp.int32(1), IDX_BITS - 1 - p)
            cnt = count(lambda blk, idx: jnp.where(blk == thr, jnp.where(idx < cand, 1, 0), 0))
            return jnp.where(cnt < need, cand, x)
        return lax.fori_loop(0, IDX_BITS, index_pass, jnp.zeros((1, tq), I32))

    last = lax.cond(jnp.max(tied) > 0, tie_break, lambda: jnp.full((1, tq), seq, I32))
    last = jnp.where(tied > 0, last, seq)
    eq_val = jnp.where(thr != INT_MIN, 1.0, 0.0)

    def write_mask(c, carry):
        r0 = pl.multiple_of(c * rc, rc)
        blk = keys_ref[pl.ds(r0, rc), :]
        idx = r0 + lax.broadcasted_iota(I32, (rc, tq), 0)
        tie_sel = jnp.where(idx <= last, eq_val, 0.0)
        mask_ref[pl.ds(r0, rc), :] = jnp.where(blk > thr, 1.0, jnp.where(blk == thr, tie_sel, 0.0))
        return carry

    lax.fori_loop(0, nchunk, write_mask, 0)

    def clear_rest(c, carry):
        mask_ref[pl.ds(pl.multiple_of(c * rc, rc), rc), :] = jnp.zeros((rc, tq), F32)
        return carry

    lax.fori_loop(nchunk, seq // rc, clear_rest, 0)


def _indexer_mask(ikw, proj_a, iq_col_block, seq, k_top):
    return pl.pallas_call(
        functools.partial(_indexer_kernel, seq=seq, k_top=k_top),
        grid=(seq // IDX_TQ,),
        in_specs=[pl.BlockSpec((seq, LANES), lambda b: (0, 0)),
                  pl.BlockSpec((IDX_TQ, N_IDX_HEADS * LANES), lambda b: (b, iq_col_block))],
        out_specs=pl.BlockSpec((seq, IDX_TQ), lambda b: (0, b)),
        out_shape=jax.ShapeDtypeStruct((seq, seq), F32),
        scratch_shapes=[pltpu.VMEM((seq, IDX_TQ), I32)],
        compiler_params=_cparams(("parallel",), 48),
        name="indexer_topk_mask",
    )(ikw, proj_a)


ATT_TQ = 512
ATT_KT = 512


def _attn_kernel(q_ref, k_ref, v_ref, mask_ref, o_ref, m_ref, l_ref, acc_ref):
    i = pl.program_id(0)
    j = pl.program_id(1)

    @pl.when(j == 0)
    def _():
        m_ref[...] = jnp.full(m_ref.shape, NEG_BIG, F32)
        l_ref[...] = jnp.zeros(l_ref.shape, F32)
        acc_ref[...] = jnp.zeros(acc_ref.shape, F32)

    @pl.when(j <= i)
    def _():
        sel = jnp.transpose(mask_ref[...]) > 0.5
        for h in range(N_HEADS_ATTN):
            cs = slice(h * HEAD_DIM, (h + 1) * HEAD_DIM)
            s = lax.dot_general(q_ref[:, cs], k_ref[:, cs], (((1,), (1,)), ((), ())),
                                preferred_element_type=F32)
            s = jnp.where(sel, s, NEG_BIG)
            m_prev = m_ref[h]
            m_new = jnp.maximum(m_prev, jnp.max(s, axis=1, keepdims=True))
            alpha = jnp.exp(m_prev - m_new)
            p = jnp.where(sel, jnp.exp(s - m_new[:, :1]), 0.0)
            l_ref[h] = alpha * l_ref[h] + jnp.sum(p, axis=1, keepdims=True)
            m_ref[h] = m_new
            acc_ref[:, cs] = alpha * acc_ref[:, cs] + jnp.dot(
                p.astype(BF16), v_ref[:, cs], preferred_element_type=F32)

    @pl.when(j == i)
    def _():
        for h in range(N_HEADS_ATTN):
            cs = slice(h * HEAD_DIM, (h + 1) * HEAD_DIM)
            o_ref[:, cs] = (acc_ref[:, cs] / l_ref[h]).astype(o_ref.dtype)


def _masked_attention(proj_a, mask_t, q_cb, k_cb, v_cb, seq):
    tq, kt = ATT_TQ, ATT_KT
    return pl.pallas_call(
        _attn_kernel,
        grid=(seq // tq, seq // kt),
        in_specs=[pl.BlockSpec((tq, A_WIDTH), lambda i, j: (i, q_cb)),
                  pl.BlockSpec((kt, A_WIDTH), lambda i, j: (jnp.minimum(i, j), k_cb)),
                  pl.BlockSpec((kt, A_WIDTH), lambda i, j: (jnp.minimum(i, j), v_cb)),
                  pl.BlockSpec((kt, tq), lambda i, j: (jnp.minimum(i, j), i))],
        out_specs=pl.BlockSpec((tq, A_WIDTH), lambda i, j: (i, 0)),
        out_shape=jax.ShapeDtypeStruct((seq, A_WIDTH), BF16),
        scratch_shapes=[pltpu.VMEM((N_HEADS_ATTN, tq, LANES), F32),
                        pltpu.VMEM((N_HEADS_ATTN, tq, LANES), F32),
                        pltpu.VMEM((tq, A_WIDTH), F32)],
        compiler_params=_cparams(("parallel", "arbitrary"), 40),
        name="masked_attention",
    )(proj_a, proj_a, proj_a, mask_t)


def _retention_kernel(q_ref, k_ref, v_ref, g_ref, idec_ref, qdec_ref, kdec_ref, cdec_ref,
                      o_ref, state_ref):
    @pl.when(pl.program_id(0) == 0)
    def _():
        state_ref[...] = jnp.zeros(state_ref.shape, F32)

    for h in range(N_RET_HEADS):
        ck = slice(h * RET_QK_DIM, (h + 1) * RET_QK_DIM)
        cv = slice(h * RET_V_DIM, (h + 1) * RET_V_DIM)
        q = q_ref[:, ck]
        k = k_ref[:, ck]
        v = v_ref[:, cv]
        scores = lax.dot_general(q, k, (((1,), (1,)), ((), ())),
                                 preferred_element_type=F32) * idec_ref[h]
        inner = jnp.dot(scores.astype(BF16), v, preferred_element_type=F32)
        state = state_ref[h]
        cross = jnp.dot(q, state.astype(BF16), preferred_element_type=F32) * qdec_ref[h]
        y = inner + cross
        mu = jnp.mean(y, axis=-1, keepdims=True)
        var = jnp.mean(jnp.square(y - mu), axis=-1, keepdims=True)
        yn = (y - mu) * lax.rsqrt(var + EPS)
        g = g_ref[:, cv]
        o_ref[:, cv] = (g * jax.nn.sigmoid(g) * yn).astype(o_ref.dtype)
        k_dec_t = jnp.transpose(k.astype(F32) * kdec_ref[h]).astype(BF16)
        state_ref[h] = state * cdec_ref[h] + jnp.dot(k_dec_t, v, preferred_element_type=F32)


def _retention(proj_a, proj_b, q_cb, k_cb, v_cb, g_cb, seq):
    c = RET_CHUNK
    h = N_RET_HEADS
    log_gamma = jnp.log1p(-jnp.exp2(-5.0 - jnp.arange(h, dtype=F32)))
    i = jnp.arange(c, dtype=F32)
    diff = i[:, None] - i[None, :]
    idec = jnp.where(diff[None] >= 0,
                     jnp.exp(jnp.maximum(diff, 0.0)[None] * log_gamma[:, None, None]), 0.0)
    kdec = jnp.exp((c - 1 - i)[None, :] * log_gamma[:, None])
    qdec = jnp.exp((i + 1)[None, :] * log_gamma[:, None])
    cdec = jnp.exp(c * log_gamma)
    qdec_b = jnp.broadcast_to(qdec[:, :, None], (h, c, RET_V_DIM))
    kdec_b = jnp.broadcast_to(kdec[:, :, None], (h, c, RET_QK_DIM))
    cdec_b = jnp.broadcast_to(cdec[:, None, None], (h, RET_QK_DIM, RET_V_DIM))
    const = lambda shape: pl.BlockSpec(shape, lambda n: (0, 0, 0))
    return pl.pallas_call(
        _retention_kernel,
        grid=(seq // c,),
        in_specs=[pl.BlockSpec((c, RET_QK_WIDTH), lambda n: (n, q_cb)),
                  pl.BlockSpec((c, RET_QK_WIDTH), lambda n: (n, k_cb)),
                  pl.BlockSpec((c, RET_V_WIDTH), lambda n: (n, v_cb)),
                  pl.BlockSpec((c, RET_V_WIDTH), lambda n: (n, g_cb)),
                  const((h, c, c)), const((h, c, RET_V_DIM)), const((h, c, RET_QK_DIM)),
                  const((h, RET_QK_DIM, RET_V_DIM))],
        out_specs=pl.BlockSpec((c, RET_V_WIDTH), lambda n: (n, 0)),
        out_shape=jax.ShapeDtypeStruct((seq, RET_V_WIDTH), BF16),
        scratch_shapes=[pltpu.VMEM((h, RET_QK_DIM, RET_V_DIM), F32)],
        compiler_params=_cparams(("arbitrary",), 24),
        name="retention",
    )(proj_a, proj_a, proj_a, proj_b, idec, qdec_b, kdec_b, cdec_b)


def _mem_attn_kernel(q_ref, kv_ref, o_ref):
    for h in range(N_MEM_HEADS):
        cs = slice(h * MEM_HEAD_DIM, (h + 1) * MEM_HEAD_DIM)
        vs = slice(MEM_WIDTH + h * MEM_HEAD_DIM, MEM_WIDTH + (h + 1) * MEM_HEAD_DIM)
        s = lax.dot_general(q_ref[:, cs], kv_ref[:, cs], (((1,), (1,)), ((), ())),
                            preferred_element_type=F32)
        e = jnp.exp(s - jnp.max(s, axis=-1, keepdims=True))
        p = e / jnp.sum(e, axis=-1, keepdims=True)
        o_ref[:, cs] = jnp.dot(p.astype(BF16), kv_ref[:, vs],
                               preferred_element_type=F32).astype(o_ref.dtype)


def _mem_attention(proj_a, mem_kv, q_cb, seq, tm):
    n_mem = mem_kv.shape[0]
    return pl.pallas_call(
        _mem_attn_kernel,
        grid=(seq // tm,),
        in_specs=[pl.BlockSpec((tm, MEM_WIDTH), lambda i: (i, q_cb)),
                  pl.BlockSpec((n_mem, 2 * MEM_WIDTH), lambda i: (0, 0))],
        out_specs=pl.BlockSpec((tm, MEM_WIDTH), lambda i: (i, 0)),
        out_shape=jax.ShapeDtypeStruct((seq, MEM_WIDTH), BF16),
        compiler_params=_cparams(("parallel",), 24),
        name="memory_attention",
    )(proj_a, mem_kv)


def _merge_kernel(oa_ref, ob_ref, oc_ref, wa_ref, wb_ref, wc_ref, g0_ref, g1_ref, g2_ref, o_ref):
    mixed = g0_ref[...] * jnp.dot(oa_ref[...], wa_ref[...], preferred_element_type=F32)
    mixed = mixed + g1_ref[...] * jnp.dot(ob_ref[...], wb_ref[...], preferred_element_type=F32)
    mixed = mixed + g2_ref[...] * jnp.dot(oc_ref[...], wc_ref[...], preferred_element_type=F32)
    o_ref[...] = mixed.astype(o_ref.dtype)


def _merge(o_a, o_b, o_c, w_a, w_b, w_c, proj_b, gate_cb0, tm, tn):
    seq = o_a.shape[0]
    d = w_a.shape[1]
    per_gate = d // tn
    act = lambda w: pl.BlockSpec((tm, w), lambda i, j: (i, 0))
    wgt = lambda k: pl.BlockSpec((k, tn), lambda i, j: (0, j))
    gate = lambda g: pl.BlockSpec((tm, tn), lambda i, j: (i, gate_cb0 + g * per_gate + j))
    return pl.pallas_call(
        _merge_kernel,
        grid=(seq // tm, d // tn),
        in_specs=[act(o_a.shape[1]), act(o_b.shape[1]), act(o_c.shape[1]),
                  wgt(w_a.shape[0]), wgt(w_b.shape[0]), wgt(w_c.shape[0]),
                  gate(0), gate(1), gate(2)],
        out_specs=pl.BlockSpec((tm, tn), lambda i, j: (i, j)),
        out_shape=jax.ShapeDtypeStruct((seq, d), BF16),
        compiler_params=_cparams(("parallel", "arbitrary"), 48),
        name="gated_merge",
    )(o_a, o_b, o_c, w_a, w_b, w_c, proj_b, proj_b, proj_b)


def _mm_norm_res_kernel(a_ref, w_ref, x_ref, g_ref, g2_ref, o_ref, h_ref, acc_ref):
    kk = pl.program_id(1)

    @pl.when(kk == 0)
    def _():
        acc_ref[...] = jnp.zeros(acc_ref.shape, F32)

    acc_ref[...] += jnp.dot(a_ref[...], w_ref[...], preferred_element_type=F32)

    @pl.when(kk == pl.num_programs(1) - 1)
    def _():
        x_new = x_ref[...] + _rms(acc_ref[...], g_ref[...])
        o_ref[...] = x_new
        h_ref[...] = _rms(x_new, g2_ref[...]).astype(h_ref.dtype)


def _mm_norm_res(a, w, x, g, g2, tm, tk, vmem_mb, name):
    m, k = a.shape
    d = w.shape[1]
    return pl.pallas_call(
        _mm_norm_res_kernel,
        grid=(m // tm, k // tk),
        in_specs=[pl.BlockSpec((tm, tk), lambda i, kk: (i, kk)),
                  pl.BlockSpec((tk, d), lambda i, kk: (kk, 0)),
                  pl.BlockSpec((tm, d), lambda i, kk: (i, 0)),
                  pl.BlockSpec((1, d), lambda i, kk: (0, 0)),
                  pl.BlockSpec((1, d), lambda i, kk: (0, 0))],
        out_specs=[pl.BlockSpec((tm, d), lambda i, kk: (i, 0)),
                   pl.BlockSpec((tm, d), lambda i, kk: (i, 0))],
        out_shape=[jax.ShapeDtypeStruct((m, d), F32), jax.ShapeDtypeStruct((m, d), BF16)],
        scratch_shapes=[pltpu.VMEM((tm, d), F32)],
        compiler_params=_cparams(("parallel", "arbitrary"), vmem_mb),
        name=name,
    )(a, w, x, g.reshape(1, d), g2.reshape(1, d))


def _swiglu_kernel(a_ref, wg_ref, wu_ref, o_ref):
    gate = jnp.dot(a_ref[...], wg_ref[...], preferred_element_type=F32)
    up = jnp.dot(a_ref[...], wu_ref[...], preferred_element_type=F32)
    o_ref[...] = (gate * jax.nn.sigmoid(gate) * up).astype(o_ref.dtype)


def _swiglu(a, w, tm, tn):
    m, k = a.shape
    d_ff = w.shape[1] // 2
    nt = d_ff // tn
    return pl.pallas_call(
        _swiglu_kernel,
        grid=(m // tm, nt),
        in_specs=[pl.BlockSpec((tm, k), lambda i, j: (i, 0)),
                  pl.BlockSpec((k, tn), lambda i, j: (0, j)),
                  pl.BlockSpec((k, tn), lambda i, j: (0, nt + j))],
        out_specs=pl.BlockSpec((tm, tn), lambda i, j: (i, j)),
        out_shape=jax.ShapeDtypeStruct((m, d_ff), BF16),
        compiler_params=_cparams(("parallel", "arbitrary"), 40),
        name="swiglu_in",
    )(a, w, w)


def _layer(x, mem, positions, g_pre_mix, g_mem, w_in, w_mem_kv, w_branch_a, w_branch_b,
           w_branch_c, w_out, g_post_mix, g_pre_ffn, w_ffn_in, w_ffn_out, g_post_ffn):
    seq, d = x.shape
    k_top = min(TOPK_MAX, seq // 4)

    sizes = (A_WIDTH, A_WIDTH, A_WIDTH, IDX_Q_WIDTH, IDX_DIM, N_IDX_HEADS,
             RET_QK_WIDTH, RET_QK_WIDTH, RET_V_WIDTH, RET_V_WIDTH, MEM_WIDTH, 3 * d)
    offs = np.concatenate([[0], np.cumsum(sizes)])
    seg = [w_in[:, offs[t]:offs[t + 1]] for t in range(len(sizes))]
    (w_aq, w_ak, w_av, w_iq, w_ik, w_iw, w_rq, w_rk, w_rv, w_rg, w_mq, w_gate) = seg
    w_iq_pad = jnp.pad(w_iq.reshape(d, N_IDX_HEADS, IDX_DIM),
                       ((0, 0), (0, 0), (0, LANES - IDX_DIM))).reshape(d, N_IDX_HEADS * LANES)
    w_ikw = jnp.pad(jnp.concatenate([w_ik, w_iw], axis=1),
                    ((0, 0), (0, LANES - IDX_DIM - N_IDX_HEADS)))
    w_a = jnp.concatenate([w_iq_pad, w_rv, w_aq, w_ak, w_av, w_rq, w_rk, w_mq], axis=1).astype(BF16)
    w_b = jnp.concatenate([w_rg, w_gate], axis=1).astype(BF16)

    tabs = _rope_tables(positions, 1024)
    (c128, s128, c64, sa64, sb64, ck, sak, sbk) = tabs
    t128 = (c128, s128, s128)
    t64 = (c64, sa64, sb64)

    h = _rmsnorm(x, g_pre_mix, 512)

    tn = 1024
    modes_a = ([(ROPE64, IDX_DIM ** -0.5)] * 2 + [(PLAIN, 1.0)] * 2
               + [(ROPE128, HEAD_DIM ** -0.5), (ROPE128, 1.0), (PLAIN, 1.0),
                  (ROPE128, 1.0), (ROPE128, RET_QK_DIM ** -0.5), (PLAIN, MEM_HEAD_DIM ** -0.5)])
    proj_iq = _proj(h, w_a[:, :2 * tn], t64, modes_a[:2], BF16, 1024, tn, 40, "proj_index_q")
    proj_a = _proj(h, w_a[:, 2 * tn:], t128, modes_a[2:], BF16, 1024, tn, 40, "proj_bf16")
    proj_b = _proj(h, w_b, t128, [(PLAIN, 1.0)] * 2 + [(SIGMOID, 1.0)] * 6, F32, 1024, tn, 48,
                   "proj_f32")
    ikw = _proj(h, w_ikw.astype(BF16), (ck, sak, sbk), [(ROPE64, 1.0)], F32, 1024, LANES, 24,
                "proj_index_kw")

    mask_t = _indexer_mask(ikw, proj_iq, 0, seq, k_top)
    o_a = _masked_attention(proj_a, mask_t, 2, 3, 4, seq)

    o_b = _retention(proj_a, proj_b, 5, 6, 0, 0, seq)

    mem_n = _rmsnorm(mem, g_mem, mem.shape[0])
    mem_kv = _proj(mem_n, w_mem_kv.astype(BF16), t128, [(PLAIN, 1.0)] * 2, BF16, mem.shape[0], tn,
                   24, "proj_mem_kv")
    o_c = _mem_attention(proj_a, mem_kv, 7, seq, 512)

    mixed = _merge(o_a, o_b, o_c, w_branch_a.astype(BF16), w_branch_b.astype(BF16),
                   w_branch_c.astype(BF16), proj_b, 2, 512, tn)
    x1, h2 = _mm_norm_res(mixed, w_out.astype(BF16), x, g_post_mix, g_pre_ffn, 512, d, 48,
                          "out_proj_norm_res")

    act = _swiglu(h2, w_ffn_in.astype(BF16), 1024, 512)
    d_ff = act.shape[1]
    x2, _ = _mm_norm_res(act, w_ffn_out.astype(BF16), x1, g_post_ffn, g_post_ffn, 512, d_ff // 4, 48,
                         "ffn_out_norm_res")
    return x2


def kernel(x, mem, positions, g_pre_mix, g_mem, w_in, w_mem_kv, w_branch_a, w_branch_b, w_branch_c,
           w_out, g_post_mix, g_pre_ffn, w_ffn_in, w_ffn_out, g_post_ffn):
    assert x.shape[0] == 1 and g_pre_mix.shape[0] == 1
    out = _layer(x[0], mem[0], positions[0], g_pre_mix[0], g_mem[0], w_in[0], w_mem_kv[0],
                 w_branch_a[0], w_branch_b[0], w_branch_c[0], w_out[0], g_post_mix[0],
                 g_pre_ffn[0], w_ffn_in[0], w_ffn_out[0], g_post_ffn[0])
    return out[None]
```

```python
import functools

import jax
import jax.numpy as jnp
import numpy as np
from jax import lax
from jax.experimental import pallas as pl
from jax.experimental.pallas import tpu as pltpu

D_MODEL = 2048
N_HEADS_ATTN = 8
HEAD_DIM = 128
N_IDX_HEADS = 16
IDX_DIM = 64
TOPK_MAX = 256
N_RET_HEADS = 8
RET_QK_DIM = 128
RET_V_DIM = 256
RET_CHUNK = 128
N_MEM_HEADS = 4
MEM_HEAD_DIM = 256
ROPE_THETA = 10000.0
EPS = 1e-6

A_WIDTH = N_HEADS_ATTN * HEAD_DIM
IDX_Q_WIDTH = N_IDX_HEADS * IDX_DIM
RET_QK_WIDTH = N_RET_HEADS * RET_QK_DIM
RET_V_WIDTH = N_RET_HEADS * RET_V_DIM
MEM_WIDTH = N_MEM_HEADS * MEM_HEAD_DIM

LANES = 128
SUBLANES = 8
VMEM_BYTES_V7X = 64 * 1024 * 1024
INT_MIN = -(2 ** 31)
NEG_BIG = -1e30
LOG2E = 1.4426950408889634

F32 = jnp.float32
BF16 = jnp.bfloat16
I32 = jnp.int32

PLAIN, ROPE128, ROPE64, SIGMOID = 0, 1, 2, 3


def _cparams(sem, vmem_mb):
    assert vmem_mb * 1024 * 1024 < VMEM_BYTES_V7X
    return pltpu.CompilerParams(dimension_semantics=sem, vmem_limit_bytes=vmem_mb * 1024 * 1024)


def _rms(x, g):
    return x * lax.rsqrt(jnp.mean(x * x, axis=-1, keepdims=True) + EPS) * g


def _rmsnorm_kernel(x_ref, g_ref, o_ref):
    o_ref[...] = _rms(x_ref[...], g_ref[...]).astype(o_ref.dtype)


def _rmsnorm(x, g, tm):
    n, d = x.shape
    return pl.pallas_call(
        _rmsnorm_kernel,
        grid=(n // tm,),
        in_specs=[pl.BlockSpec((tm, d), lambda i: (i, 0)), pl.BlockSpec((1, d), lambda i: (0, 0))],
        out_specs=pl.BlockSpec((tm, d), lambda i: (i, 0)),
        out_shape=jax.ShapeDtypeStruct((n, d), BF16),
        compiler_params=_cparams(("parallel",), 24),
        name="rmsnorm",
    )(x, g.reshape(1, d))


def _rope_table_kernel(pos_ref, c_ref, c128_ref, s128_ref, c64_ref, sa64_ref, sb64_ref,
                       ck_ref, sak_ref, sbk_ref):
    pos = pos_ref[...]
    ang128 = pos * c_ref[0:1, :]
    ang64 = pos * c_ref[1:2, :]
    c128_ref[...] = jnp.cos(ang128)
    s128_ref[...] = jnp.sin(ang128) * c_ref[2:3, :]
    cos64 = jnp.cos(ang64)
    sin64 = jnp.sin(ang64)
    sa = sin64 * c_ref[3:4, :]
    sb = sin64 * c_ref[4:5, :]
    c64_ref[...] = cos64
    sa64_ref[...] = sa
    sb64_ref[...] = sb
    ck_ref[...] = cos64 * c_ref[5:6, :] + c_ref[6:7, :]
    sak_ref[...] = sa * c_ref[5:6, :]
    sbk_ref[...] = sb * c_ref[5:6, :]


def _rope_tables(positions, tm):
    s = positions.shape[-1]
    pos_b = jnp.broadcast_to(positions.reshape(s, 1).astype(F32), (s, LANES))
    lane = np.arange(LANES)
    f128 = ROPE_THETA ** (-jnp.arange(0, HEAD_DIM, 2, dtype=F32) / HEAD_DIM)
    f64 = ROPE_THETA ** (-jnp.arange(0, IDX_DIM, 2, dtype=F32) / IDX_DIM)
    half64 = (lane % IDX_DIM) < IDX_DIM // 2
    rows = [
        jnp.tile(f128, 2),
        jnp.tile(f64, 4),
        jnp.asarray(np.where(lane < HEAD_DIM // 2, -1.0, 1.0), F32),
        jnp.asarray(np.where(half64, -1.0, 0.0), F32),
        jnp.asarray(np.where(half64, 0.0, 1.0), F32),
        jnp.asarray(np.where(lane < IDX_DIM, 1.0, 0.0), F32),
        jnp.asarray(np.where(lane < IDX_DIM, 0.0,
                             np.where(lane < IDX_DIM + N_IDX_HEADS, N_IDX_HEADS ** -0.5, 1.0)), F32),
        jnp.zeros((LANES,), F32),
    ]
    consts = jnp.stack(rows)
    spec = pl.BlockSpec((tm, LANES), lambda i: (i, 0))
    return pl.pallas_call(
        _rope_table_kernel,
        grid=(s // tm,),
        in_specs=[spec, pl.BlockSpec((SUBLANES, LANES), lambda i: (0, 0))],
        out_specs=[spec] * 8,
        out_shape=[jax.ShapeDtypeStruct((s, LANES), F32)] * 8,
        compiler_params=_cparams(("parallel",), 24),
        name="rope_tables",
    )(pos_b, consts)


def _rope128(x, cos, sin_signed):
    return x * cos + pltpu.roll(x, HEAD_DIM // 2, 1) * sin_signed


def _rope64(x, cos, sin_a, sin_b):
    return (x * cos + pltpu.roll(x, LANES - IDX_DIM // 2, 1) * sin_a
            + pltpu.roll(x, IDX_DIM // 2, 1) * sin_b)


PROJ_CHUNK = 2 * LANES


def _proj_kernel(a_ref, b_ref, t0_ref, t1_ref, t2_ref, o_ref, *, mode, scales, tn):
    j = pl.program_id(1)
    scale = jnp.float32(scales[-1])
    for t in range(len(scales) - 2, -1, -1):
        scale = jnp.where(j == t, jnp.float32(scales[t]), scale)
    uniform = all(s == scales[0] for s in scales)
    cw = min(PROJ_CHUNK, tn)
    for c in range(tn // cw):
        acc = jnp.dot(a_ref[...], b_ref[:, c * cw:(c + 1) * cw], preferred_element_type=F32)
        for s in range(cw // LANES):
            x = acc[:, s * LANES:(s + 1) * LANES]
            if mode == ROPE128:
                x = _rope128(x, t0_ref[...], t1_ref[...])
            elif mode == ROPE64:
                x = _rope64(x, t0_ref[...], t1_ref[...], t2_ref[...])
            elif mode == SIGMOID:
                x = jax.nn.sigmoid(x)
            if not (uniform and scales[0] == 1.0):
                x = x * scale
            col = c * cw + s * LANES
            o_ref[:, col:col + LANES] = x.astype(o_ref.dtype)


def _proj(a, b, tables, mode, scales, out_dtype, tm, tn, vmem_mb, name):
    m, k = a.shape
    n = b.shape[1]
    assert n // tn == len(scales) and tn % min(PROJ_CHUNK, tn) == 0
    tspec = pl.BlockSpec((tm, LANES), lambda i, j: (i, 0))
    return pl.pallas_call(
        functools.partial(_proj_kernel, mode=mode, scales=tuple(scales), tn=tn),
        grid=(m // tm, n // tn),
        in_specs=[pl.BlockSpec((tm, k), lambda i, j: (i, 0)),
                  pl.BlockSpec((k, tn), lambda i, j: (0, j)), tspec, tspec, tspec],
        out_specs=pl.BlockSpec((tm, tn), lambda i, j: (i, j)),
        out_shape=jax.ShapeDtypeStruct((m, n), out_dtype),
        compiler_params=_cparams(("parallel", "arbitrary"), vmem_mb),
        name=name,
    )(a, b, *tables)


IDX_TQ = 256
IDX_KT = 512
IDX_PARTS = 4
IDX_BITS = 13
NEG_INF_ORDER = 0x007FFFFF


def _order_to_f32(u):
    key = u ^ INT_MIN
    bits = key ^ (lax.shift_right_arithmetic(key, 31) & jnp.int32(0x7FFFFFFF))
    return lax.bitcast_convert_type(bits, F32)


def _indexer_kernel(ikw_ref, iq_ref, mask_ref, s_ref, *, seq, k_top):
    b = pl.program_id(0)
    tq, kt = IDX_TQ, IDX_KT
    pr = kt // IDX_PARTS
    q0 = b * tq
    ntile = q0 // kt + 1

    w_t = jnp.transpose(ikw_ref[pl.ds(pl.multiple_of(q0, tq), tq), :])
    qidx = q0 + lax.broadcasted_iota(I32, (kt, tq), 1)

    def score_tile(t, carry):
        r0 = pl.multiple_of(t * kt, kt)
        ik = ikw_ref[pl.ds(r0, kt), :].astype(BF16)
        acc = jnp.zeros((kt, tq), F32)
        for h in range(N_IDX_HEADS):
            qh = iq_ref[:, h * LANES:(h + 1) * LANES]
            s = lax.dot_general(ik, qh, (((1,), (1,)), ((), ())), preferred_element_type=F32)
            acc = acc + jnp.maximum(s, 0.0) * w_t[IDX_DIM + h:IDX_DIM + h + 1, :]
        kidx = r0 + lax.broadcasted_iota(I32, (kt, tq), 0)
        s_ref[pl.ds(r0, kt), :] = jnp.where(kidx <= qidx, acc, -jnp.inf)
        return carry

    lax.fori_loop(0, ntile, score_tile, 0)

    def count(pred):
        def body(t, cnt):
            parts = []
            for part in range(IDX_PARTS):
                r0 = pl.multiple_of(t * kt + part * pr, pr)
                idx = r0 + lax.broadcasted_iota(I32, (pr, tq), 0)
                hit = pred(s_ref[pl.ds(r0, pr), :], idx)
                parts.append(jnp.sum(hit.reshape(pr // SUBLANES, SUBLANES, tq), axis=0))
            return cnt + ((parts[0] + parts[1]) + (parts[2] + parts[3]))
        cnt = lax.fori_loop(0, ntile, body, jnp.zeros((SUBLANES, tq), I32))
        return jnp.sum(cnt, axis=0, keepdims=True)

    def value_pass(p, thr_u):
        cand_u = thr_u | lax.shift_left(jnp.int32(1), 31 - p)
        cand = _order_to_f32(cand_u)
        cnt = count(lambda sc, idx: jnp.where(sc >= cand, 1, 0))
        below = jnp.where(cand_u >= 0, jnp.where(cand != cand, 1, 0), 0)
        return jnp.where(cnt + below * k_top >= k_top, cand_u, thr_u)

    thr_u = lax.fori_loop(0, 32, value_pass, jnp.zeros((1, tq), I32))
    thr = _order_to_f32(thr_u)
    cnt_gt = count(lambda sc, idx: jnp.where(sc > thr, 1, 0))
    cnt_eq = count(lambda sc, idx: jnp.where(sc == thr, 1, 0))
    need = k_top - cnt_gt
    finite = jnp.where(thr_u != NEG_INF_ORDER, 1, 0)
    tied = finite * jnp.where(cnt_eq > need, 1, 0)

    def tie_break():
        def index_pass(p, x):
            cand = x | lax.shift_left(jnp.int32(1), IDX_BITS - 1 - p)
            cnt = count(lambda sc, idx: jnp.where(sc == thr, jnp.where(idx < cand, 1, 0), 0))
            return jnp.where(cnt < need, cand, x)
        return lax.fori_loop(0, IDX_BITS, index_pass, jnp.zeros((1, tq), I32))

    last = lax.cond(jnp.max(tied) > 0, tie_break, lambda: jnp.full((1, tq), seq, I32))
    last = jnp.where(tied > 0, last, seq)
    eq_val = finite.astype(F32)

    def write_mask(t, carry):
        r0 = pl.multiple_of(t * kt, kt)
        sc = s_ref[pl.ds(r0, kt), :]
        idx = r0 + lax.broadcasted_iota(I32, (kt, tq), 0)
        tie_sel = jnp.where(idx <= last, eq_val, 0.0)
        mask_ref[pl.ds(r0, kt), :] = jnp.where(sc > thr, 1.0, jnp.where(sc == thr, tie_sel, 0.0))
        return carry

    lax.fori_loop(0, ntile, write_mask, 0)

    def clear_rest(t, carry):
        mask_ref[pl.ds(pl.multiple_of(t * kt, kt), kt), :] = jnp.zeros((kt, tq), F32)
        return carry

    lax.fori_loop(ntile, seq // kt, clear_rest, 0)


def _indexer_mask(ikw, proj_a, iq_col_block, seq, k_top):
    return pl.pallas_call(
        functools.partial(_indexer_kernel, seq=seq, k_top=k_top),
        grid=(seq // IDX_TQ,),
        in_specs=[pl.BlockSpec((seq, LANES), lambda b: (0, 0)),
                  pl.BlockSpec((IDX_TQ, N_IDX_HEADS * LANES), lambda b: (b, iq_col_block))],
        out_specs=pl.BlockSpec((seq, IDX_TQ), lambda b: (0, b)),
        out_shape=jax.ShapeDtypeStruct((seq, seq), F32),
        scratch_shapes=[pltpu.VMEM((seq, IDX_TQ), F32)],
        compiler_params=_cparams(("parallel",), 48),
        name="indexer_topk_mask",
    )(ikw, proj_a)


ATT_TQ = 512
ATT_KT = 512


def _attn_kernel(q_ref, k_ref, v_ref, mask_ref, o_ref, m_ref, l_ref, acc_ref):
    i = pl.program_id(0)
    j = pl.program_id(1)

    @pl.when(j == 0)
    def _():
        m_ref[...] = jnp.full(m_ref.shape, NEG_BIG, F32)
        l_ref[...] = jnp.zeros(l_ref.shape, F32)
        acc_ref[...] = jnp.zeros(acc_ref.shape, F32)

    @pl.when(j <= i)
    def _():
        sel = jnp.transpose(mask_ref[...]) > 0.5
        for h in range(N_HEADS_ATTN):
            cs = slice(h * HEAD_DIM, (h + 1) * HEAD_DIM)
            s = lax.dot_general(q_ref[:, cs], k_ref[:, cs], (((1,), (1,)), ((), ())),
                                preferred_element_type=F32)
            s = jnp.where(sel, s, NEG_BIG)
            m_prev = m_ref[h]
            m_new = jnp.maximum(m_prev, jnp.max(s, axis=1, keepdims=True))
            alpha = jnp.exp2(m_prev - m_new)
            p = jnp.exp2(s - m_new[:, :1])
            l_ref[h] = alpha * l_ref[h] + jnp.sum(p, axis=1, keepdims=True)
            m_ref[h] = m_new
            acc_ref[:, cs] = alpha * acc_ref[:, cs] + jnp.dot(
                p.astype(BF16), v_ref[:, cs], preferred_element_type=F32)

    @pl.when(j == i)
    def _():
        for h in range(N_HEADS_ATTN):
            cs = slice(h * HEAD_DIM, (h + 1) * HEAD_DIM)
            o_ref[:, cs] = (acc_ref[:, cs] / l_ref[h]).astype(o_ref.dtype)


def _masked_attention(proj_qk, proj_v, mask_t, q_cb, k_cb, v_cb, seq):
    tq, kt = ATT_TQ, ATT_KT
    return pl.pallas_call(
        _attn_kernel,
        grid=(seq // tq, seq // kt),
        in_specs=[pl.BlockSpec((tq, A_WIDTH), lambda i, j: (i, q_cb)),
                  pl.BlockSpec((kt, A_WIDTH), lambda i, j: (jnp.minimum(i, j), k_cb)),
                  pl.BlockSpec((kt, A_WIDTH), lambda i, j: (jnp.minimum(i, j), v_cb)),
                  pl.BlockSpec((kt, tq), lambda i, j: (jnp.minimum(i, j), i))],
        out_specs=pl.BlockSpec((tq, A_WIDTH), lambda i, j: (i, 0)),
        out_shape=jax.ShapeDtypeStruct((seq, A_WIDTH), BF16),
        scratch_shapes=[pltpu.VMEM((N_HEADS_ATTN, tq, LANES), F32),
                        pltpu.VMEM((N_HEADS_ATTN, tq, LANES), F32),
                        pltpu.VMEM((tq, A_WIDTH), F32)],
        compiler_params=_cparams(("parallel", "arbitrary"), 40),
        name="masked_attention",
    )(proj_qk, proj_qk, proj_v, mask_t)


def _retention_kernel(q_ref, k_ref, v_ref, g_ref, idec_ref, qdec_ref, kdec_ref, cdec_ref,
                      o_ref, state_ref):
    @pl.when(pl.program_id(0) == 0)
    def _():
        state_ref[...] = jnp.zeros(state_ref.shape, F32)

    for h in range(N_RET_HEADS):
        ck = slice(h * RET_QK_DIM, (h + 1) * RET_QK_DIM)
        cv = slice(h * RET_V_DIM, (h + 1) * RET_V_DIM)
        q = q_ref[:, ck]
        k = k_ref[:, ck]
        v = v_ref[:, cv]
        scores = lax.dot_general(q, k, (((1,), (1,)), ((), ())),
                                 preferred_element_type=F32) * idec_ref[h]
        inner = jnp.dot(scores.astype(BF16), v, preferred_element_type=F32)
        state = state_ref[h]
        cross = jnp.dot(q, state.astype(BF16), preferred_element_type=F32) * qdec_ref[h]
        y = inner + cross
        mu = jnp.mean(y, axis=-1, keepdims=True)
        var = jnp.mean(jnp.square(y - mu), axis=-1, keepdims=True)
        yn = (y - mu) * lax.rsqrt(var + EPS)
        g = g_ref[:, cv]
        o_ref[:, cv] = (g * jax.nn.sigmoid(g) * yn).astype(o_ref.dtype)
        k_dec_t = jnp.transpose(k.astype(F32) * kdec_ref[h]).astype(BF16)
        state_ref[h] = state * cdec_ref[h] + jnp.dot(k_dec_t, v, preferred_element_type=F32)


def _retention(proj_qk, proj_v, proj_g, q_cb, k_cb, v_cb, g_cb, seq):
    c = RET_CHUNK
    h = N_RET_HEADS
    log_gamma = jnp.log1p(-jnp.exp2(-5.0 - jnp.arange(h, dtype=F32)))
    i = jnp.arange(c, dtype=F32)
    diff = i[:, None] - i[None, :]
    idec = jnp.where(diff[None] >= 0,
                     jnp.exp(jnp.maximum(diff, 0.0)[None] * log_gamma[:, None, None]), 0.0)
    kdec = jnp.exp((c - 1 - i)[None, :] * log_gamma[:, None])
    qdec = jnp.exp((i + 1)[None, :] * log_gamma[:, None])
    cdec = jnp.exp(c * log_gamma)
    qdec_b = jnp.broadcast_to(qdec[:, :, None], (h, c, RET_V_DIM))
    kdec_b = jnp.broadcast_to(kdec[:, :, None], (h, c, RET_QK_DIM))
    cdec_b = jnp.broadcast_to(cdec[:, None, None], (h, RET_QK_DIM, RET_V_DIM))
    const = lambda shape: pl.BlockSpec(shape, lambda n: (0, 0, 0))
    return pl.pallas_call(
        _retention_kernel,
        grid=(seq // c,),
        in_specs=[pl.BlockSpec((c, RET_QK_WIDTH), lambda n: (n, q_cb)),
                  pl.BlockSpec((c, RET_QK_WIDTH), lambda n: (n, k_cb)),
                  pl.BlockSpec((c, RET_V_WIDTH), lambda n: (n, v_cb)),
                  pl.BlockSpec((c, RET_V_WIDTH), lambda n: (n, g_cb)),
                  const((h, c, c)), const((h, c, RET_V_DIM)), const((h, c, RET_QK_DIM)),
                  const((h, RET_QK_DIM, RET_V_DIM))],
        out_specs=pl.BlockSpec((c, RET_V_WIDTH), lambda n: (n, 0)),
        out_shape=jax.ShapeDtypeStruct((seq, RET_V_WIDTH), BF16),
        scratch_shapes=[pltpu.VMEM((h, RET_QK_DIM, RET_V_DIM), F32)],
        compiler_params=_cparams(("arbitrary",), 24),
        name="retention",
    )(proj_qk, proj_qk, proj_v, proj_g, idec, qdec_b, kdec_b, cdec_b)


def _mem_attn_kernel(q_ref, kv_ref, o_ref):
    for h in range(N_MEM_HEADS):
        cs = slice(h * MEM_HEAD_DIM, (h + 1) * MEM_HEAD_DIM)
        vs = slice(MEM_WIDTH + h * MEM_HEAD_DIM, MEM_WIDTH + (h + 1) * MEM_HEAD_DIM)
        s = lax.dot_general(q_ref[:, cs], kv_ref[:, cs], (((1,), (1,)), ((), ())),
                            preferred_element_type=F32)
        e = jnp.exp(s - jnp.max(s, axis=-1, keepdims=True))
        p = e / jnp.sum(e, axis=-1, keepdims=True)
        o_ref[:, cs] = jnp.dot(p.astype(BF16), kv_ref[:, vs],
                               preferred_element_type=F32).astype(o_ref.dtype)


def _mem_attention(proj_a, mem_kv, q_cb, seq, tm):
    n_mem = mem_kv.shape[0]
    return pl.pallas_call(
        _mem_attn_kernel,
        grid=(seq // tm,),
        in_specs=[pl.BlockSpec((tm, MEM_WIDTH), lambda i: (i, q_cb)),
                  pl.BlockSpec((n_mem, 2 * MEM_WIDTH), lambda i: (0, 0))],
        out_specs=pl.BlockSpec((tm, MEM_WIDTH), lambda i: (i, 0)),
        out_shape=jax.ShapeDtypeStruct((seq, MEM_WIDTH), BF16),
        compiler_params=_cparams(("parallel",), 24),
        name="memory_attention",
    )(proj_a, mem_kv)


def _merge_kernel(oa_ref, ob_ref, oc_ref, wa_ref, wb_ref, wc_ref, g0_ref, g1_ref, g2_ref, o_ref):
    mixed = g0_ref[...] * jnp.dot(oa_ref[...], wa_ref[...], preferred_element_type=F32)
    mixed = mixed + g1_ref[...] * jnp.dot(ob_ref[...], wb_ref[...], preferred_element_type=F32)
    mixed = mixed + g2_ref[...] * jnp.dot(oc_ref[...], wc_ref[...], preferred_element_type=F32)
    o_ref[...] = mixed.astype(o_ref.dtype)


def _merge(o_a, o_b, o_c, w_a, w_b, w_c, proj_b, gate_cb0, tm, tn):
    seq = o_a.shape[0]
    d = w_a.shape[1]
    per_gate = d // tn
    act = lambda w: pl.BlockSpec((tm, w), lambda i, j: (i, 0))
    wgt = lambda k: pl.BlockSpec((k, tn), lambda i, j: (0, j))
    gate = lambda g: pl.BlockSpec((tm, tn), lambda i, j: (i, gate_cb0 + g * per_gate + j))
    return pl.pallas_call(
        _merge_kernel,
        grid=(seq // tm, d // tn),
        in_specs=[act(o_a.shape[1]), act(o_b.shape[1]), act(o_c.shape[1]),
                  wgt(w_a.shape[0]), wgt(w_b.shape[0]), wgt(w_c.shape[0]),
                  gate(0), gate(1), gate(2)],
        out_specs=pl.BlockSpec((tm, tn), lambda i, j: (i, j)),
        out_shape=jax.ShapeDtypeStruct((seq, d), BF16),
        compiler_params=_cparams(("parallel", "arbitrary"), 48),
        name="gated_merge",
    )(o_a, o_b, o_c, w_a, w_b, w_c, proj_b, proj_b, proj_b)


def _mm_norm_res_kernel(a_ref, w_ref, x_ref, g_ref, *rest, with_next_norm):
    if with_next_norm:
        g2_ref, o_ref, h_ref = rest
    else:
        (o_ref,) = rest
    d = w_ref.shape[1]
    cols = [slice(c, c + PROJ_CHUNK) for c in range(0, d, PROJ_CHUNK)]
    accs = [jnp.dot(a_ref[...], w_ref[:, cs], preferred_element_type=F32) for cs in cols]
    ssq = sum(jnp.sum(acc * acc, axis=-1, keepdims=True) for acc in accs)
    inv = lax.rsqrt(ssq / d + EPS)
    x_new = [x_ref[:, cs] + acc * inv * g_ref[:, cs] for cs, acc in zip(cols, accs)]
    for cs, xc in zip(cols, x_new):
        o_ref[:, cs] = xc
    if with_next_norm:
        ssq2 = sum(jnp.sum(xc * xc, axis=-1, keepdims=True) for xc in x_new)
        inv2 = lax.rsqrt(ssq2 / d + EPS)
        for cs, xc in zip(cols, x_new):
            h_ref[:, cs] = (xc * inv2 * g2_ref[:, cs]).astype(h_ref.dtype)


def _mm_norm_res(a, w, x, g, g2, tm, vmem_mb, name):
    m, k = a.shape
    d = w.shape[1]
    row = lambda width: pl.BlockSpec((tm, width), lambda i: (i, 0))
    vec = pl.BlockSpec((1, d), lambda i: (0, 0))
    with_next_norm = g2 is not None
    in_specs = [row(k), pl.BlockSpec((k, d), lambda i: (0, 0), pipeline_mode=pl.Buffered(1)),
                row(d), vec]
    args = [a, w, x, g.reshape(1, d)]
    out_specs = [row(d)]
    out_shape = [jax.ShapeDtypeStruct((m, d), F32)]
    if with_next_norm:
        in_specs.append(vec)
        args.append(g2.reshape(1, d))
        out_specs.append(row(d))
        out_shape.append(jax.ShapeDtypeStruct((m, d), BF16))
    return pl.pallas_call(
        functools.partial(_mm_norm_res_kernel, with_next_norm=with_next_norm),
        grid=(m // tm,),
        in_specs=in_specs,
        out_specs=out_specs,
        out_shape=out_shape,
        compiler_params=_cparams(("parallel",), vmem_mb),
        name=name,
    )(*args)


def _swiglu_kernel(a_ref, wg_ref, wu_ref, o_ref):
    gate = jnp.dot(a_ref[...], wg_ref[...], preferred_element_type=F32)
    up = jnp.dot(a_ref[...], wu_ref[...], preferred_element_type=F32)
    o_ref[...] = (gate * jax.nn.sigmoid(gate) * up).astype(o_ref.dtype)


def _swiglu(a, w, tm, tn):
    m, k = a.shape
    d_ff = w.shape[1] // 2
    nt = d_ff // tn
    return pl.pallas_call(
        _swiglu_kernel,
        grid=(m // tm, nt),
        in_specs=[pl.BlockSpec((tm, k), lambda i, j: (i, 0)),
                  pl.BlockSpec((k, tn), lambda i, j: (0, j)),
                  pl.BlockSpec((k, tn), lambda i, j: (0, nt + j))],
        out_specs=pl.BlockSpec((tm, tn), lambda i, j: (i, j)),
        out_shape=jax.ShapeDtypeStruct((m, d_ff), BF16),
        compiler_params=_cparams(("parallel", "arbitrary"), 40),
        name="swiglu_in",
    )(a, w, w)


def _layer(x, mem, positions, g_pre_mix, g_mem, w_in, w_mem_kv, w_branch_a, w_branch_b,
           w_branch_c, w_out, g_post_mix, g_pre_ffn, w_ffn_in, w_ffn_out, g_post_ffn):
    seq, d = x.shape
    k_top = min(TOPK_MAX, seq // 4)

    sizes = (A_WIDTH, A_WIDTH, A_WIDTH, IDX_Q_WIDTH, IDX_DIM, N_IDX_HEADS,
             RET_QK_WIDTH, RET_QK_WIDTH, RET_V_WIDTH, RET_V_WIDTH, MEM_WIDTH, 3 * d)
    offs = np.concatenate([[0], np.cumsum(sizes)])
    seg = [w_in[:, offs[t]:offs[t + 1]] for t in range(len(sizes))]
    (w_aq, w_ak, w_av, w_iq, w_ik, w_iw, w_rq, w_rk, w_rv, w_rg, w_mq, w_gate) = seg
    w_iq_pad = jnp.pad(w_iq.reshape(d, N_IDX_HEADS, IDX_DIM),
                       ((0, 0), (0, 0), (0, LANES - IDX_DIM))).reshape(d, N_IDX_HEADS * LANES)
    w_ikw = jnp.pad(jnp.concatenate([w_ik, w_iw], axis=1),
                    ((0, 0), (0, LANES - IDX_DIM - N_IDX_HEADS)))
    cat = lambda ws: jnp.concatenate(ws, axis=1).astype(BF16)

    tabs = _rope_tables(positions, 1024)
    (c128, s128, c64, sa64, sb64, ck, sak, sbk) = tabs
    t128 = (c128, s128, s128)
    t64 = (c64, sa64, sb64)

    h = _rmsnorm(x, g_pre_mix, 512)

    tn = 1024
    proj_iq = _proj(h, w_iq_pad.astype(BF16), t64, ROPE64, [IDX_DIM ** -0.5] * 2, BF16, 1024, tn,
                    40, "proj_index_q")
    proj_r = _proj(h, cat([w_aq, w_ak, w_rq, w_rk]), t128, ROPE128,
                   [HEAD_DIM ** -0.5 * LOG2E, 1.0, 1.0, RET_QK_DIM ** -0.5], BF16, 1024, tn, 40,
                   "proj_rope")
    proj_p = _proj(h, cat([w_rv, w_av, w_mq]), t128, PLAIN, [1.0, 1.0, 1.0, MEM_HEAD_DIM ** -0.5],
                   BF16, 1024, tn, 40, "proj_plain")
    gates = _proj(h, w_gate.astype(BF16), t128, SIGMOID, [1.0] * 6, F32, 1024, tn, 48, "proj_gates")
    r_g = _proj(h, w_rg.astype(BF16), t128, PLAIN, [1.0] * 2, F32, 1024, tn, 48, "proj_ret_gate")
    ikw = _proj(h, w_ikw.astype(BF16), (ck, sak, sbk), ROPE64, [1.0], F32, 1024, LANES, 24,
                "proj_index_kw")

    mask_t = _indexer_mask(ikw, proj_iq, 0, seq, k_top)
    o_a = _masked_attention(proj_r, proj_p, mask_t, 0, 1, 2, seq)

    o_b = _retention(proj_r, proj_p, r_g, 2, 3, 0, 0, seq)

    mem_n = _rmsnorm(mem, g_mem, mem.shape[0])
    mem_kv = _proj(mem_n, w_mem_kv.astype(BF16), t128, PLAIN, [1.0] * 2, BF16, mem.shape[0], tn,
                   24, "proj_mem_kv")
    o_c = _mem_attention(proj_p, mem_kv, 3, seq, 512)

    mixed = _merge(o_a, o_b, o_c, w_branch_a.astype(BF16), w_branch_b.astype(BF16),
                   w_branch_c.astype(BF16), gates, 0, 512, tn)
    x1, h2 = _mm_norm_res(mixed, w_out.astype(BF16), x, g_post_mix, g_pre_ffn, 512, 40,
                          "out_proj_norm_res")

    act = _swiglu(h2, w_ffn_in.astype(BF16), 1024, 512)
    (x2,) = _mm_norm_res(act, w_ffn_out.astype(BF16), x1, g_post_ffn, None, 256, 48,
                         "ffn_out_norm_res")
    return x2


def kernel(x, mem, positions, g_pre_mix, g_mem, w_in, w_mem_kv, w_branch_a, w_branch_b, w_branch_c,
           w_out, g_post_mix, g_pre_ffn, w_ffn_in, w_ffn_out, g_post_ffn):
    assert x.shape[0] == 1 and g_pre_mix.shape[0] == 1
    out = _layer(x[0], mem[0], positions[0], g_pre_mix[0], g_mem[0], w_in[0], w_mem_kv[0],
                 w_branch_a[0], w_branch_b[0], w_branch_c[0], w_out[0], g_post_mix[0],
                 g_pre_ffn[0], w_ffn_in[0], w_ffn_out[0], g_post_ffn[0])
    return out[None]
```

```python
import functools

import jax
import jax.numpy as jnp
import numpy as np
from jax import lax
from jax.experimental import pallas as pl
from jax.experimental.pallas import tpu as pltpu

D_MODEL = 2048
N_HEADS_ATTN = 8
HEAD_DIM = 128
N_IDX_HEADS = 16
IDX_DIM = 64
TOPK_MAX = 256
N_RET_HEADS = 8
RET_QK_DIM = 128
RET_V_DIM = 256
RET_CHUNK = 128
N_MEM_HEADS = 4
MEM_HEAD_DIM = 256
ROPE_THETA = 10000.0
EPS = 1e-6

A_WIDTH = N_HEADS_ATTN * HEAD_DIM
IDX_Q_WIDTH = N_IDX_HEADS * IDX_DIM
RET_QK_WIDTH = N_RET_HEADS * RET_QK_DIM
RET_V_WIDTH = N_RET_HEADS * RET_V_DIM
MEM_WIDTH = N_MEM_HEADS * MEM_HEAD_DIM

LANES = 128
SUBLANES = 8
VMEM_BYTES_V7X = 64 * 1024 * 1024
INT_MIN = -(2 ** 31)
NEG_BIG = -1e30
LOG2E = 1.4426950408889634

F32 = jnp.float32
BF16 = jnp.bfloat16
I32 = jnp.int32

PLAIN, ROPE128, ROPE64, SIGMOID = 0, 1, 2, 3


def _cparams(sem, vmem_mb):
    assert vmem_mb * 1024 * 1024 < VMEM_BYTES_V7X
    return pltpu.CompilerParams(dimension_semantics=sem, vmem_limit_bytes=vmem_mb * 1024 * 1024)


def _rms(x, g):
    return x * lax.rsqrt(jnp.mean(x * x, axis=-1, keepdims=True) + EPS) * g


def _rmsnorm_kernel(x_ref, g_ref, o_ref):
    o_ref[...] = _rms(x_ref[...], g_ref[...]).astype(o_ref.dtype)


def _rmsnorm(x, g, tm):
    n, d = x.shape
    return pl.pallas_call(
        _rmsnorm_kernel,
        grid=(n // tm,),
        in_specs=[pl.BlockSpec((tm, d), lambda i: (i, 0)), pl.BlockSpec((1, d), lambda i: (0, 0))],
        out_specs=pl.BlockSpec((tm, d), lambda i: (i, 0)),
        out_shape=jax.ShapeDtypeStruct((n, d), BF16),
        compiler_params=_cparams(("parallel",), 24),
        name="rmsnorm",
    )(x, g.reshape(1, d))


def _rope_table_kernel(pos_ref, c_ref, c128_ref, s128_ref, c64_ref, sa64_ref, sb64_ref,
                       ck_ref, sak_ref, sbk_ref):
    pos = pos_ref[...]
    ang128 = pos * c_ref[0:1, :]
    ang64 = pos * c_ref[1:2, :]
    c128_ref[...] = jnp.cos(ang128)
    s128_ref[...] = jnp.sin(ang128) * c_ref[2:3, :]
    cos64 = jnp.cos(ang64)
    sin64 = jnp.sin(ang64)
    sa = sin64 * c_ref[3:4, :]
    sb = sin64 * c_ref[4:5, :]
    c64_ref[...] = cos64
    sa64_ref[...] = sa
    sb64_ref[...] = sb
    ck_ref[...] = cos64 * c_ref[5:6, :] + c_ref[6:7, :]
    sak_ref[...] = sa * c_ref[5:6, :]
    sbk_ref[...] = sb * c_ref[5:6, :]


def _rope_tables(positions, tm):
    s = positions.shape[-1]
    pos_b = jnp.broadcast_to(positions.reshape(s, 1).astype(F32), (s, LANES))
    lane = np.arange(LANES)
    f128 = ROPE_THETA ** (-jnp.arange(0, HEAD_DIM, 2, dtype=F32) / HEAD_DIM)
    f64 = ROPE_THETA ** (-jnp.arange(0, IDX_DIM, 2, dtype=F32) / IDX_DIM)
    half64 = (lane % IDX_DIM) < IDX_DIM // 2
    rows = [
        jnp.tile(f128, 2),
        jnp.tile(f64, 4),
        jnp.asarray(np.where(lane < HEAD_DIM // 2, -1.0, 1.0), F32),
        jnp.asarray(np.where(half64, -1.0, 0.0), F32),
        jnp.asarray(np.where(half64, 0.0, 1.0), F32),
        jnp.asarray(np.where(lane < IDX_DIM, 1.0, 0.0), F32),
        jnp.asarray(np.where(lane < IDX_DIM, 0.0,
                             np.where(lane < IDX_DIM + N_IDX_HEADS, N_IDX_HEADS ** -0.5, 1.0)), F32),
        jnp.zeros((LANES,), F32),
    ]
    consts = jnp.stack(rows)
    spec = pl.BlockSpec((tm, LANES), lambda i: (i, 0))
    return pl.pallas_call(
        _rope_table_kernel,
        grid=(s // tm,),
        in_specs=[spec, pl.BlockSpec((SUBLANES, LANES), lambda i: (0, 0))],
        out_specs=[spec] * 8,
        out_shape=[jax.ShapeDtypeStruct((s, LANES), F32)] * 8,
        compiler_params=_cparams(("parallel",), 24),
        name="rope_tables",
    )(pos_b, consts)


def _rope128(x, cos, sin_signed):
    return x * cos + pltpu.roll(x, HEAD_DIM // 2, 1) * sin_signed


def _rope64(x, cos, sin_a, sin_b):
    return (x * cos + pltpu.roll(x, LANES - IDX_DIM // 2, 1) * sin_a
            + pltpu.roll(x, IDX_DIM // 2, 1) * sin_b)


PROJ_CHUNK = 2 * LANES


def _proj_kernel(a_ref, bt_ref, t0_ref, t1_ref, t2_ref, o_ref, *, mode, scales, tn):
    j = pl.program_id(1)
    scale = jnp.float32(scales[-1])
    for t in range(len(scales) - 2, -1, -1):
        scale = jnp.where(j == t, jnp.float32(scales[t]), scale)
    uniform = all(s == scales[0] for s in scales)
    cw = min(PROJ_CHUNK, tn)
    for c in range(tn // cw):
        acc = lax.dot_general(a_ref[...], bt_ref[c * cw:(c + 1) * cw, :], (((1,), (1,)), ((), ())),
                              preferred_element_type=F32)
        for s in range(cw // LANES):
            x = acc[:, s * LANES:(s + 1) * LANES]
            if mode == ROPE128:
                x = _rope128(x, t0_ref[...], t1_ref[...])
            elif mode == ROPE64:
                x = _rope64(x, t0_ref[...], t1_ref[...], t2_ref[...])
            elif mode == SIGMOID:
                x = jax.nn.sigmoid(x)
            if not (uniform and scales[0] == 1.0):
                x = x * scale
            col = c * cw + s * LANES
            o_ref[:, col:col + LANES] = x.astype(o_ref.dtype)


def _proj(a, bt, tables, mode, scales, out_dtype, tm, tn, vmem_mb, name):
    m, k = a.shape
    n = bt.shape[0]
    assert bt.shape[1] == k and n // tn == len(scales) and tn % min(PROJ_CHUNK, tn) == 0
    tspec = pl.BlockSpec((tm, LANES), lambda i, j: (i, 0))
    return pl.pallas_call(
        functools.partial(_proj_kernel, mode=mode, scales=tuple(scales), tn=tn),
        grid=(m // tm, n // tn),
        in_specs=[pl.BlockSpec((tm, k), lambda i, j: (i, 0)),
                  pl.BlockSpec((tn, k), lambda i, j: (j, 0)), tspec, tspec, tspec],
        out_specs=pl.BlockSpec((tm, tn), lambda i, j: (i, j)),
        out_shape=jax.ShapeDtypeStruct((m, n), out_dtype),
        compiler_params=_cparams(("parallel", "arbitrary"), vmem_mb),
        name=name,
    )(a, bt, *tables)


IDX_TQ = 256
IDX_KT = 512
IDX_PARTS = 4
IDX_BITS = 13
NEG_INF_ORDER = 0x007FFFFF


def _order_to_f32(u):
    key = u ^ INT_MIN
    bits = key ^ (lax.shift_right_arithmetic(key, 31) & jnp.int32(0x7FFFFFFF))
    return lax.bitcast_convert_type(bits, F32)


def _indexer_kernel(ikw_ref, iq_ref, mask_ref, s_ref, *, seq, k_top):
    b = pl.program_id(0)
    tq, kt = IDX_TQ, IDX_KT
    pr = kt // IDX_PARTS
    q0 = b * tq
    ntile = q0 // kt + 1

    w_t = jnp.transpose(ikw_ref[pl.ds(pl.multiple_of(q0, tq), tq), :])
    qidx = q0 + lax.broadcasted_iota(I32, (kt, tq), 1)

    def score_tile(t, carry):
        r0 = pl.multiple_of(t * kt, kt)
        ik = ikw_ref[pl.ds(r0, kt), :].astype(BF16)
        acc = jnp.zeros((kt, tq), F32)
        for h in range(N_IDX_HEADS):
            qh = iq_ref[:, h * LANES:(h + 1) * LANES]
            s = lax.dot_general(ik, qh, (((1,), (1,)), ((), ())), preferred_element_type=F32)
            acc = acc + jnp.maximum(s, 0.0) * w_t[IDX_DIM + h:IDX_DIM + h + 1, :]
        kidx = r0 + lax.broadcasted_iota(I32, (kt, tq), 0)
        s_ref[pl.ds(r0, kt), :] = jnp.where(kidx <= qidx, acc, -jnp.inf)
        return carry

    lax.fori_loop(0, ntile, score_tile, 0)

    def count(pred):
        def body(t, cnt):
            parts = []
            for part in range(IDX_PARTS):
                r0 = pl.multiple_of(t * kt + part * pr, pr)
                idx = r0 + lax.broadcasted_iota(I32, (pr, tq), 0)
                hit = pred(s_ref[pl.ds(r0, pr), :], idx)
                parts.append(jnp.sum(hit.reshape(pr // SUBLANES, SUBLANES, tq), axis=0))
            return cnt + ((parts[0] + parts[1]) + (parts[2] + parts[3]))
        cnt = lax.fori_loop(0, ntile, body, jnp.zeros((SUBLANES, tq), I32))
        return jnp.sum(cnt, axis=0, keepdims=True)

    def value_pass(carry):
        p, thr_u, split, _ = carry
        cand_u = thr_u | lax.shift_left(jnp.int32(1), 31 - p)
        cand = _order_to_f32(cand_u)
        cnt = count(lambda sc, idx: jnp.where(sc >= cand, 1, 0))
        below = jnp.where(cand_u >= 0, jnp.where(cand != cand, 1, 0), 0)
        thr_u = jnp.where(cnt + below * k_top >= k_top, cand_u, thr_u)
        split = jnp.maximum(split, jnp.where(cnt == k_top, 1, 0))
        return p + 1, thr_u, split, jnp.min(split)

    init = (jnp.int32(0), jnp.zeros((1, tq), I32), jnp.zeros((1, tq), I32), jnp.int32(0))
    _, thr_u, _, _ = lax.while_loop(lambda c: jnp.logical_and(c[0] < 32, c[3] == 0), value_pass, init)
    thr = _order_to_f32(thr_u)
    cnt_gt = count(lambda sc, idx: jnp.where(sc > thr, 1, 0))
    cnt_eq = count(lambda sc, idx: jnp.where(sc == thr, 1, 0))
    need = k_top - cnt_gt
    finite = jnp.where(thr_u != NEG_INF_ORDER, 1, 0)
    tied = finite * jnp.where(cnt_eq > need, 1, 0)

    def tie_break():
        def index_pass(p, x):
            cand = x | lax.shift_left(jnp.int32(1), IDX_BITS - 1 - p)
            cnt = count(lambda sc, idx: jnp.where(sc == thr, jnp.where(idx < cand, 1, 0), 0))
            return jnp.where(cnt < need, cand, x)
        return lax.fori_loop(0, IDX_BITS, index_pass, jnp.zeros((1, tq), I32))

    last = lax.cond(jnp.max(tied) > 0, tie_break, lambda: jnp.full((1, tq), seq, I32))
    last = jnp.where(tied > 0, last, seq)
    eq_val = finite.astype(F32)

    def write_mask(t, carry):
        r0 = pl.multiple_of(t * kt, kt)
        sc = s_ref[pl.ds(r0, kt), :]
        idx = r0 + lax.broadcasted_iota(I32, (kt, tq), 0)
        tie_sel = jnp.where(idx <= last, eq_val, 0.0)
        sel = jnp.where(sc > thr, 1.0, jnp.where(sc == thr, tie_sel, 0.0))
        mask_ref[t] = jnp.transpose(sel).astype(mask_ref.dtype)
        return carry

    lax.fori_loop(0, ntile, write_mask, 0)

    def clear_rest(t, carry):
        mask_ref[t] = jnp.zeros((tq, kt), mask_ref.dtype)
        return carry

    lax.fori_loop(ntile, seq // kt, clear_rest, 0)


def _indexer_mask(ikw, proj_a, iq_col_block, seq, k_top):
    return pl.pallas_call(
        functools.partial(_indexer_kernel, seq=seq, k_top=k_top),
        grid=(seq // IDX_TQ,),
        in_specs=[pl.BlockSpec((seq, LANES), lambda b: (0, 0)),
                  pl.BlockSpec((IDX_TQ, N_IDX_HEADS * LANES), lambda b: (b, iq_col_block))],
        out_specs=pl.BlockSpec((seq // IDX_KT, IDX_TQ, IDX_KT), lambda b: (0, b, 0)),
        out_shape=jax.ShapeDtypeStruct((seq // IDX_KT, seq, IDX_KT), BF16),
        scratch_shapes=[pltpu.VMEM((seq, IDX_TQ), F32)],
        compiler_params=_cparams(("parallel",), 48),
        name="indexer_topk_mask",
    )(ikw, proj_a)


ATT_TQ = 512
ATT_KT = 512


def _attn_kernel(q_ref, k_ref, v_ref, mask_ref, o_ref, m_ref, l_ref, acc_ref):
    i = pl.program_id(0)
    j = pl.program_id(1)

    @pl.when(j == 0)
    def _():
        m_ref[...] = jnp.full(m_ref.shape, NEG_BIG, F32)
        l_ref[...] = jnp.zeros(l_ref.shape, F32)
        acc_ref[...] = jnp.zeros(acc_ref.shape, F32)

    @pl.when(j <= i)
    def _():
        sel = mask_ref[0].astype(F32) > 0.5
        for h in range(N_HEADS_ATTN):
            cs = slice(h * HEAD_DIM, (h + 1) * HEAD_DIM)
            s = lax.dot_general(q_ref[:, cs], k_ref[:, cs], (((1,), (1,)), ((), ())),
                                preferred_element_type=F32)
            s = jnp.where(sel, s, NEG_BIG)
            m_prev = m_ref[h]
            m_new = jnp.maximum(m_prev, jnp.max(s, axis=1, keepdims=True))
            alpha = jnp.exp2(m_prev - m_new)
            p = jnp.exp2(s - jnp.concatenate([m_new] * (s.shape[1] // LANES), axis=1))
            l_ref[h] = alpha * l_ref[h] + jnp.sum(p, axis=1, keepdims=True)
            m_ref[h] = m_new
            acc_ref[:, cs] = alpha * acc_ref[:, cs] + jnp.dot(
                p.astype(BF16), v_ref[:, cs], preferred_element_type=F32)

    @pl.when(j == i)
    def _():
        for h in range(N_HEADS_ATTN):
            cs = slice(h * HEAD_DIM, (h + 1) * HEAD_DIM)
            o_ref[:, cs] = (acc_ref[:, cs] / l_ref[h]).astype(o_ref.dtype)


def _masked_attention(proj_qk, proj_v, mask, q_cb, k_cb, v_cb, seq):
    assert ATT_KT == IDX_KT and mask.shape == (seq // ATT_KT, seq, ATT_KT)
    tq, kt = ATT_TQ, ATT_KT
    return pl.pallas_call(
        _attn_kernel,
        grid=(seq // tq, seq // kt),
        in_specs=[pl.BlockSpec((tq, A_WIDTH), lambda i, j: (i, q_cb)),
                  pl.BlockSpec((kt, A_WIDTH), lambda i, j: (jnp.minimum(i, j), k_cb)),
                  pl.BlockSpec((kt, A_WIDTH), lambda i, j: (jnp.minimum(i, j), v_cb)),
                  pl.BlockSpec((1, tq, kt), lambda i, j: (jnp.minimum(i, j), i, 0))],
        out_specs=pl.BlockSpec((tq, A_WIDTH), lambda i, j: (i, 0)),
        out_shape=jax.ShapeDtypeStruct((seq, A_WIDTH), BF16),
        scratch_shapes=[pltpu.VMEM((N_HEADS_ATTN, tq, LANES), F32),
                        pltpu.VMEM((N_HEADS_ATTN, tq, LANES), F32),
                        pltpu.VMEM((tq, A_WIDTH), F32)],
        compiler_params=_cparams(("parallel", "arbitrary"), 40),
        name="masked_attention",
    )(proj_qk, proj_qk, proj_v, mask)


def _retention_kernel(q_ref, k_ref, v_ref, g_ref, idec_ref, qdec_ref, kdec_ref, cdec_ref,
                      o_ref, state_ref):
    @pl.when(pl.program_id(0) == 0)
    def _():
        state_ref[...] = jnp.zeros(state_ref.shape, F32)

    for h in range(N_RET_HEADS):
        ck = slice(h * RET_QK_DIM, (h + 1) * RET_QK_DIM)
        cv = slice(h * RET_V_DIM, (h + 1) * RET_V_DIM)
        q = q_ref[:, ck]
        k = k_ref[:, ck]
        v = v_ref[:, cv]
        scores = lax.dot_general(q, k, (((1,), (1,)), ((), ())),
                                 preferred_element_type=F32) * idec_ref[h]
        inner = jnp.dot(scores.astype(BF16), v, preferred_element_type=F32)
        state = state_ref[h]
        cross = jnp.dot(q, state.astype(BF16), preferred_element_type=F32) * qdec_ref[h]
        y = inner + cross
        mu = jnp.mean(y, axis=-1, keepdims=True)
        var = jnp.mean(jnp.square(y - mu), axis=-1, keepdims=True)
        yn = (y - mu) * lax.rsqrt(var + EPS)
        g = g_ref[:, cv]
        o_ref[:, cv] = (g * jax.nn.sigmoid(g) * yn).astype(o_ref.dtype)
        k_dec_t = jnp.transpose(k.astype(F32) * kdec_ref[h]).astype(BF16)
        state_ref[h] = state * cdec_ref[h] + jnp.dot(k_dec_t, v, preferred_element_type=F32)


def _retention(proj_qk, proj_v, proj_g, q_cb, k_cb, v_cb, g_cb, seq):
    c = RET_CHUNK
    h = N_RET_HEADS
    log_gamma = jnp.log1p(-jnp.exp2(-5.0 - jnp.arange(h, dtype=F32)))
    i = jnp.arange(c, dtype=F32)
    diff = i[:, None] - i[None, :]
    idec = jnp.where(diff[None] >= 0,
                     jnp.exp(jnp.maximum(diff, 0.0)[None] * log_gamma[:, None, None]), 0.0)
    kdec = jnp.exp((c - 1 - i)[None, :] * log_gamma[:, None])
    qdec = jnp.exp((i + 1)[None, :] * log_gamma[:, None])
    cdec = jnp.exp(c * log_gamma)
    qdec_b = jnp.broadcast_to(qdec[:, :, None], (h, c, RET_V_DIM))
    kdec_b = jnp.broadcast_to(kdec[:, :, None], (h, c, RET_QK_DIM))
    cdec_b = jnp.broadcast_to(cdec[:, None, None], (h, RET_QK_DIM, RET_V_DIM))
    const = lambda shape: pl.BlockSpec(shape, lambda n: (0, 0, 0))
    return pl.pallas_call(
        _retention_kernel,
        grid=(seq // c,),
        in_specs=[pl.BlockSpec((c, RET_QK_WIDTH), lambda n: (n, q_cb)),
                  pl.BlockSpec((c, RET_QK_WIDTH), lambda n: (n, k_cb)),
                  pl.BlockSpec((c, RET_V_WIDTH), lambda n: (n, v_cb)),
                  pl.BlockSpec((c, RET_V_WIDTH), lambda n: (n, g_cb)),
                  const((h, c, c)), const((h, c, RET_V_DIM)), const((h, c, RET_QK_DIM)),
                  const((h, RET_QK_DIM, RET_V_DIM))],
        out_specs=pl.BlockSpec((c, RET_V_WIDTH), lambda n: (n, 0)),
        out_shape=jax.ShapeDtypeStruct((seq, RET_V_WIDTH), BF16),
        scratch_shapes=[pltpu.VMEM((h, RET_QK_DIM, RET_V_DIM), F32)],
        compiler_params=_cparams(("arbitrary",), 24),
        name="retention",
    )(proj_qk, proj_qk, proj_v, proj_g, idec, qdec_b, kdec_b, cdec_b)


def _mem_attn_kernel(q_ref, kv_ref, o_ref):
    for h in range(N_MEM_HEADS):
        cs = slice(h * MEM_HEAD_DIM, (h + 1) * MEM_HEAD_DIM)
        vs = slice(MEM_WIDTH + h * MEM_HEAD_DIM, MEM_WIDTH + (h + 1) * MEM_HEAD_DIM)
        s = lax.dot_general(q_ref[:, cs], kv_ref[:, cs], (((1,), (1,)), ((), ())),
                            preferred_element_type=F32)
        e = jnp.exp(s - jnp.max(s, axis=-1, keepdims=True))
        p = e / jnp.sum(e, axis=-1, keepdims=True)
        o_ref[:, cs] = jnp.dot(p.astype(BF16), kv_ref[:, vs],
                               preferred_element_type=F32).astype(o_ref.dtype)


def _mem_attention(proj_a, mem_kv, q_cb, seq, tm):
    n_mem = mem_kv.shape[0]
    return pl.pallas_call(
        _mem_attn_kernel,
        grid=(seq // tm,),
        in_specs=[pl.BlockSpec((tm, MEM_WIDTH), lambda i: (i, q_cb)),
                  pl.BlockSpec((n_mem, 2 * MEM_WIDTH), lambda i: (0, 0))],
        out_specs=pl.BlockSpec((tm, MEM_WIDTH), lambda i: (i, 0)),
        out_shape=jax.ShapeDtypeStruct((seq, MEM_WIDTH), BF16),
        compiler_params=_cparams(("parallel",), 24),
        name="memory_attention",
    )(proj_a, mem_kv)


def _merge_kernel(oa_ref, ob_ref, oc_ref, wa_ref, wb_ref, wc_ref, g0_ref, g1_ref, g2_ref, o_ref):
    for c in range(0, o_ref.shape[1], PROJ_CHUNK):
        cs = slice(c, c + PROJ_CHUNK)
        mixed = g0_ref[:, cs] * jnp.dot(oa_ref[...], wa_ref[:, cs], preferred_element_type=F32)
        mixed = mixed + g1_ref[:, cs] * jnp.dot(ob_ref[...], wb_ref[:, cs], preferred_element_type=F32)
        mixed = mixed + g2_ref[:, cs] * jnp.dot(oc_ref[...], wc_ref[:, cs], preferred_element_type=F32)
        o_ref[:, cs] = mixed.astype(o_ref.dtype)


def _merge(o_a, o_b, o_c, w_a, w_b, w_c, proj_b, gate_cb0, tm, tn):
    seq = o_a.shape[0]
    d = w_a.shape[1]
    per_gate = d // tn
    act = lambda w: pl.BlockSpec((tm, w), lambda i, j: (i, 0))
    wgt = lambda k: pl.BlockSpec((k, tn), lambda i, j: (0, j))
    gate = lambda g: pl.BlockSpec((tm, tn), lambda i, j: (i, gate_cb0 + g * per_gate + j))
    return pl.pallas_call(
        _merge_kernel,
        grid=(seq // tm, d // tn),
        in_specs=[act(o_a.shape[1]), act(o_b.shape[1]), act(o_c.shape[1]),
                  wgt(w_a.shape[0]), wgt(w_b.shape[0]), wgt(w_c.shape[0]),
                  gate(0), gate(1), gate(2)],
        out_specs=pl.BlockSpec((tm, tn), lambda i, j: (i, j)),
        out_shape=jax.ShapeDtypeStruct((seq, d), BF16),
        compiler_params=_cparams(("parallel", "arbitrary"), 48),
        name="gated_merge",
    )(o_a, o_b, o_c, w_a, w_b, w_c, proj_b, proj_b, proj_b)


def _mm_norm_res_kernel(a_ref, w_ref, x_ref, g_ref, *rest, with_next_norm):
    if with_next_norm:
        g2_ref, o_ref, h_ref = rest
    else:
        (o_ref,) = rest
    d = w_ref.shape[1]
    cols = [slice(c, c + PROJ_CHUNK) for c in range(0, d, PROJ_CHUNK)]
    accs = [jnp.dot(a_ref[...], w_ref[:, cs], preferred_element_type=F32) for cs in cols]
    ssq = sum(jnp.sum(acc * acc, axis=-1, keepdims=True) for acc in accs)
    inv = lax.rsqrt(ssq / d + EPS)
    x_new = [x_ref[:, cs] + acc * inv * g_ref[:, cs] for cs, acc in zip(cols, accs)]
    for cs, xc in zip(cols, x_new):
        o_ref[:, cs] = xc
    if with_next_norm:
        ssq2 = sum(jnp.sum(xc * xc, axis=-1, keepdims=True) for xc in x_new)
        inv2 = lax.rsqrt(ssq2 / d + EPS)
        for cs, xc in zip(cols, x_new):
            h_ref[:, cs] = (xc * inv2 * g2_ref[:, cs]).astype(h_ref.dtype)


def _mm_norm_res(a, w, x, g, g2, tm, vmem_mb, name):
    m, k = a.shape
    d = w.shape[1]
    row = lambda width: pl.BlockSpec((tm, width), lambda i: (i, 0))
    vec = pl.BlockSpec((1, d), lambda i: (0, 0))
    with_next_norm = g2 is not None
    in_specs = [row(k), pl.BlockSpec((k, d), lambda i: (0, 0), pipeline_mode=pl.Buffered(1)),
                row(d), vec]
    args = [a, w, x, g.reshape(1, d)]
    out_specs = [row(d)]
    out_shape = [jax.ShapeDtypeStruct((m, d), F32)]
    if with_next_norm:
        in_specs.append(vec)
        args.append(g2.reshape(1, d))
        out_specs.append(row(d))
        out_shape.append(jax.ShapeDtypeStruct((m, d), BF16))
    return pl.pallas_call(
        functools.partial(_mm_norm_res_kernel, with_next_norm=with_next_norm),
        grid=(m // tm,),
        in_specs=in_specs,
        out_specs=out_specs,
        out_shape=out_shape,
        compiler_params=_cparams(("parallel",), vmem_mb),
        name=name,
    )(*args)


def _swiglu_kernel(a_ref, wg_ref, wu_ref, o_ref):
    gate = jnp.dot(a_ref[...], wg_ref[...], preferred_element_type=F32)
    up = jnp.dot(a_ref[...], wu_ref[...], preferred_element_type=F32)
    o_ref[...] = (gate * jax.nn.sigmoid(gate) * up).astype(o_ref.dtype)


def _swiglu(a, w, tm, tn):
    m, k = a.shape
    d_ff = w.shape[1] // 2
    nt = d_ff // tn
    return pl.pallas_call(
        _swiglu_kernel,
        grid=(m // tm, nt),
        in_specs=[pl.BlockSpec((tm, k), lambda i, j: (i, 0)),
                  pl.BlockSpec((k, tn), lambda i, j: (0, j)),
                  pl.BlockSpec((k, tn), lambda i, j: (0, nt + j))],
        out_specs=pl.BlockSpec((tm, tn), lambda i, j: (i, j)),
        out_shape=jax.ShapeDtypeStruct((m, d_ff), BF16),
        compiler_params=_cparams(("parallel", "arbitrary"), 40),
        name="swiglu_in",
    )(a, w, w)


def _layer(x, mem, positions, g_pre_mix, g_mem, w_in, w_mem_kv, w_branch_a, w_branch_b,
           w_branch_c, w_out, g_post_mix, g_pre_ffn, w_ffn_in, w_ffn_out, g_post_ffn):
    seq, d = x.shape
    k_top = min(TOPK_MAX, seq // 4)

    sizes = (A_WIDTH, A_WIDTH, A_WIDTH, IDX_Q_WIDTH, IDX_DIM, N_IDX_HEADS,
             RET_QK_WIDTH, RET_QK_WIDTH, RET_V_WIDTH, RET_V_WIDTH, MEM_WIDTH, 3 * d)
    offs = np.concatenate([[0], np.cumsum(sizes)])
    w_in_t = jnp.swapaxes(w_in, 0, 1)
    seg = [w_in_t[offs[t]:offs[t + 1]] for t in range(len(sizes))]
    (w_aq, w_ak, w_av, w_iq, w_ik, w_iw, w_rq, w_rk, w_rv, w_rg, w_mq, w_gate) = seg
    w_iq_pad = jnp.pad(w_iq.reshape(N_IDX_HEADS, IDX_DIM, d),
                       ((0, 0), (0, LANES - IDX_DIM), (0, 0))).reshape(N_IDX_HEADS * LANES, d)
    w_ikw = jnp.pad(jnp.concatenate([w_ik, w_iw], axis=0),
                    ((0, LANES - IDX_DIM - N_IDX_HEADS), (0, 0)))
    cat = lambda ws: jnp.concatenate(ws, axis=0).astype(BF16)

    tabs = _rope_tables(positions, 1024)
    (c128, s128, c64, sa64, sb64, ck, sak, sbk) = tabs
    t128 = (c128, s128, s128)
    t64 = (c64, sa64, sb64)

    h = _rmsnorm(x, g_pre_mix, 512)

    tn = 1024
    proj_iq = _proj(h, w_iq_pad.astype(BF16), t64, ROPE64, [IDX_DIM ** -0.5] * 2, BF16, 1024, tn,
                    40, "proj_index_q")
    proj_r = _proj(h, cat([w_aq, w_ak, w_rq, w_rk]), t128, ROPE128,
                   [HEAD_DIM ** -0.5 * LOG2E, 1.0, 1.0, RET_QK_DIM ** -0.5], BF16, 1024, tn, 40,
                   "proj_rope")
    proj_p = _proj(h, cat([w_rv, w_av, w_mq]), t128, PLAIN, [1.0, 1.0, 1.0, MEM_HEAD_DIM ** -0.5],
                   BF16, 1024, tn, 40, "proj_plain")
    gates = _proj(h, w_gate.astype(BF16), t128, SIGMOID, [1.0] * 6, F32, 1024, tn, 48, "proj_gates")
    r_g = _proj(h, w_rg.astype(BF16), t128, PLAIN, [1.0] * 2, F32, 1024, tn, 48, "proj_ret_gate")
    ikw = _proj(h, w_ikw.astype(BF16), (ck, sak, sbk), ROPE64, [1.0], F32, 1024, LANES, 24,
                "proj_index_kw")

    mask = _indexer_mask(ikw, proj_iq, 0, seq, k_top)
    o_a = _masked_attention(proj_r, proj_p, mask, 0, 1, 2, seq)

    o_b = _retention(proj_r, proj_p, r_g, 2, 3, 0, 0, seq)

    mem_n = _rmsnorm(mem, g_mem, mem.shape[0])
    mem_kv = _proj(mem_n, jnp.swapaxes(w_mem_kv, 0, 1).astype(BF16), t128, PLAIN, [1.0] * 2, BF16,
                   mem.shape[0], tn, 24, "proj_mem_kv")
    o_c = _mem_attention(proj_p, mem_kv, 3, seq, 512)

    mixed = _merge(o_a, o_b, o_c, w_branch_a.astype(BF16), w_branch_b.astype(BF16),
                   w_branch_c.astype(BF16), gates, 0, 512, tn)
    x1, h2 = _mm_norm_res(mixed, w_out.astype(BF16), x, g_post_mix, g_pre_ffn, 512, 40,
                          "out_proj_norm_res")

    act = _swiglu(h2, w_ffn_in.astype(BF16), 1024, 512)
    (x2,) = _mm_norm_res(act, w_ffn_out.astype(BF16), x1, g_post_ffn, None, 256, 48,
                         "ffn_out_norm_res")
    return x2


def kernel(x, mem, positions, g_pre_mix, g_mem, w_in, w_mem_kv, w_branch_a, w_branch_b, w_branch_c,
           w_out, g_post_mix, g_pre_ffn, w_ffn_in, w_ffn_out, g_post_ffn):
    assert x.shape[0] == 1 and g_pre_mix.shape[0] == 1
    out = _layer(x[0], mem[0], positions[0], g_pre_mix[0], g_mem[0], w_in[0], w_mem_kv[0],
                 w_branch_a[0], w_branch_b[0], w_branch_c[0], w_out[0], g_post_mix[0],
                 g_pre_ffn[0], w_ffn_in[0], w_ffn_out[0], g_post_ffn[0])
    return out[None]
```

```python
import functools

import jax
import jax.numpy as jnp
import numpy as np
from jax import lax
from jax.experimental import pallas as pl
from jax.experimental.pallas import tpu as pltpu

D_MODEL = 2048
N_HEADS_ATTN = 8
HEAD_DIM = 128
N_IDX_HEADS = 16
IDX_DIM = 64
TOPK_MAX = 256
N_RET_HEADS = 8
RET_QK_DIM = 128
RET_V_DIM = 256
RET_CHUNK = 128
N_MEM_HEADS = 4
MEM_HEAD_DIM = 256
ROPE_THETA = 10000.0
EPS = 1e-6

A_WIDTH = N_HEADS_ATTN * HEAD_DIM
IDX_Q_WIDTH = N_IDX_HEADS * IDX_DIM
RET_QK_WIDTH = N_RET_HEADS * RET_QK_DIM
RET_V_WIDTH = N_RET_HEADS * RET_V_DIM
MEM_WIDTH = N_MEM_HEADS * MEM_HEAD_DIM

LANES = 128
SUBLANES = 8
VMEM_BYTES_V7X = 64 * 1024 * 1024
INT_MIN = -(2 ** 31)
NEG_BIG = -1e30
LOG2E = 1.4426950408889634

F32 = jnp.float32
BF16 = jnp.bfloat16
I32 = jnp.int32

PLAIN, ROPE128, ROPE64, SIGMOID = 0, 1, 2, 3


def _cparams(sem, vmem_mb):
    assert vmem_mb * 1024 * 1024 < VMEM_BYTES_V7X
    return pltpu.CompilerParams(dimension_semantics=sem, vmem_limit_bytes=vmem_mb * 1024 * 1024)


def _rms(x, g):
    return x * lax.rsqrt(jnp.mean(x * x, axis=-1, keepdims=True) + EPS) * g


def _rmsnorm_kernel(x_ref, g_ref, o_ref):
    o_ref[...] = _rms(x_ref[...], g_ref[...]).astype(o_ref.dtype)


def _rmsnorm(x, g, tm):
    n, d = x.shape
    return pl.pallas_call(
        _rmsnorm_kernel,
        grid=(n // tm,),
        in_specs=[pl.BlockSpec((tm, d), lambda i: (i, 0)), pl.BlockSpec((1, d), lambda i: (0, 0))],
        out_specs=pl.BlockSpec((tm, d), lambda i: (i, 0)),
        out_shape=jax.ShapeDtypeStruct((n, d), BF16),
        compiler_params=_cparams(("parallel",), 24),
        name="rmsnorm",
    )(x, g.reshape(1, d))


def _rope_table_kernel(pos_ref, c_ref, c128_ref, s128_ref, c64_ref, sa64_ref, sb64_ref,
                       ck_ref, sak_ref, sbk_ref):
    pos = pos_ref[...]
    ang128 = pos * c_ref[0:1, :]
    ang64 = pos * c_ref[1:2, :]
    c128_ref[...] = jnp.cos(ang128)
    s128_ref[...] = jnp.sin(ang128) * c_ref[2:3, :]
    cos64 = jnp.cos(ang64)
    sin64 = jnp.sin(ang64)
    sa = sin64 * c_ref[3:4, :]
    sb = sin64 * c_ref[4:5, :]
    c64_ref[...] = cos64
    sa64_ref[...] = sa
    sb64_ref[...] = sb
    ck_ref[...] = cos64 * c_ref[5:6, :] + c_ref[6:7, :]
    sak_ref[...] = sa * c_ref[5:6, :]
    sbk_ref[...] = sb * c_ref[5:6, :]


def _rope_tables(positions, tm):
    s = positions.shape[-1]
    pos_b = jnp.broadcast_to(positions.reshape(s, 1).astype(F32), (s, LANES))
    lane = np.arange(LANES)
    f128 = ROPE_THETA ** (-jnp.arange(0, HEAD_DIM, 2, dtype=F32) / HEAD_DIM)
    f64 = ROPE_THETA ** (-jnp.arange(0, IDX_DIM, 2, dtype=F32) / IDX_DIM)
    half64 = (lane % IDX_DIM) < IDX_DIM // 2
    rows = [
        jnp.tile(f128, 2),
        jnp.tile(f64, 4),
        jnp.asarray(np.where(lane < HEAD_DIM // 2, -1.0, 1.0), F32),
        jnp.asarray(np.where(half64, -1.0, 0.0), F32),
        jnp.asarray(np.where(half64, 0.0, 1.0), F32),
        jnp.asarray(np.where(lane < IDX_DIM, 1.0, 0.0), F32),
        jnp.asarray(np.where(lane < IDX_DIM, 0.0,
                             np.where(lane < IDX_DIM + N_IDX_HEADS, N_IDX_HEADS ** -0.5, 1.0)), F32),
        jnp.zeros((LANES,), F32),
    ]
    consts = jnp.stack(rows)
    spec = pl.BlockSpec((tm, LANES), lambda i: (i, 0))
    return pl.pallas_call(
        _rope_table_kernel,
        grid=(s // tm,),
        in_specs=[spec, pl.BlockSpec((SUBLANES, LANES), lambda i: (0, 0))],
        out_specs=[spec] * 8,
        out_shape=[jax.ShapeDtypeStruct((s, LANES), F32)] * 8,
        compiler_params=_cparams(("parallel",), 24),
        name="rope_tables",
    )(pos_b, consts)


def _rope128(x, cos, sin_signed):
    return x * cos + pltpu.roll(x, HEAD_DIM // 2, 1) * sin_signed


def _rope64(x, cos, sin_a, sin_b):
    return (x * cos + pltpu.roll(x, LANES - IDX_DIM // 2, 1) * sin_a
            + pltpu.roll(x, IDX_DIM // 2, 1) * sin_b)


PROJ_CHUNK = 2 * LANES


def _select_by_tile(j, values, dtype):
    out = jnp.asarray(values[-1], dtype)
    for t in range(len(values) - 2, -1, -1):
        out = jnp.where(j == t, jnp.asarray(values[t], dtype), out)
    return out


def _proj_kernel(a_ref, wt_ref, t0_ref, t1_ref, t2_ref, o_ref, wbf_ref, *, mode, scales, tn,
                 valid_rows, pad_heads):
    j = pl.program_id(0)

    @pl.when(pl.program_id(1) == 0)
    def _():
        if pad_heads:
            zeros = jnp.zeros((LANES - IDX_DIM, wbf_ref.shape[1]), BF16)
            for h in range(tn // LANES):
                wbf_ref[h * LANES:h * LANES + IDX_DIM, :] = (
                    wt_ref[h * IDX_DIM:(h + 1) * IDX_DIM, :].astype(BF16))
                wbf_ref[h * LANES + IDX_DIM:(h + 1) * LANES, :] = zeros
        else:
            w = wt_ref[...]
            if valid_rows < tn:
                w = jnp.where(lax.broadcasted_iota(I32, w.shape, 0) < valid_rows, w, 0.0)
            wbf_ref[...] = w.astype(BF16)

    scale = _select_by_tile(j, scales, F32)
    uniform = all(s == scales[0] for s in scales)
    cw = min(PROJ_CHUNK, tn)
    for c in range(tn // cw):
        acc = lax.dot_general(a_ref[...], wbf_ref[c * cw:(c + 1) * cw, :], (((1,), (1,)), ((), ())),
                              preferred_element_type=F32)
        for s in range(cw // LANES):
            x = acc[:, s * LANES:(s + 1) * LANES]
            if mode == ROPE128:
                x = _rope128(x, t0_ref[...], t1_ref[...])
            elif mode == ROPE64:
                x = _rope64(x, t0_ref[...], t1_ref[...], t2_ref[...])
            elif mode == SIGMOID:
                x = jax.nn.sigmoid(x)
            if not (uniform and scales[0] == 1.0):
                x = x * scale
            col = c * cw + s * LANES
            o_ref[:, col:col + LANES] = x.astype(o_ref.dtype)


def _proj(a, wt, row_offsets, tables, mode, scales, out_dtype, tm, tn, vmem_mb, name, *,
          valid_rows=None, pad_heads=False):
    m, k = a.shape
    assert wt.shape[1] == k and len(row_offsets) == len(scales) and tn % min(PROJ_CHUNK, tn) == 0
    src_rows = tn // 2 if pad_heads else tn
    valid_rows = tn if valid_rows is None else valid_rows
    assert all(o % SUBLANES == 0 and o + src_rows <= wt.shape[0] for o in row_offsets)
    tspec = pl.BlockSpec((tm, LANES), lambda j, i: (i, 0))
    return pl.pallas_call(
        functools.partial(_proj_kernel, mode=mode, scales=tuple(scales), tn=tn,
                          valid_rows=valid_rows, pad_heads=pad_heads),
        grid=(len(row_offsets), m // tm),
        in_specs=[pl.BlockSpec((tm, k), lambda j, i: (i, 0)),
                  pl.BlockSpec((pl.Element(src_rows), pl.Element(k)),
                               lambda j, i: (_select_by_tile(
                                   j, [o // SUBLANES for o in row_offsets], I32) * SUBLANES, 0)),
                  tspec, tspec, tspec],
        out_specs=pl.BlockSpec((tm, tn), lambda j, i: (i, j)),
        out_shape=jax.ShapeDtypeStruct((m, len(row_offsets) * tn), out_dtype),
        scratch_shapes=[pltpu.VMEM((tn, k), BF16)],
        compiler_params=_cparams(("arbitrary", "arbitrary"), vmem_mb),
        name=name,
    )(a, wt, *tables)


IDX_TQ = 256
IDX_KT = 512
IDX_PARTS = 4
IDX_BITS = 13
NEG_INF_ORDER = 0x007FFFFF
IDX_UNCHECKED_PASSES = 22


def _order_to_f32(u):
    key = u ^ INT_MIN
    bits = key ^ (lax.shift_right_arithmetic(key, 31) & jnp.int32(0x7FFFFFFF))
    return lax.bitcast_convert_type(bits, F32)


def _f32_to_order(x):
    bits = lax.bitcast_convert_type(x, I32)
    return bits ^ (lax.shift_right_arithmetic(bits, 31) & jnp.int32(0x7FFFFFFF)) ^ INT_MIN


def _indexer_kernel(ikw_ref, iq_ref, mask_ref, s_ref, gmax_ref, *, seq, k_top):
    b = pl.program_id(0)
    tq, kt = IDX_TQ, IDX_KT
    pr = kt // IDX_PARTS
    q0 = b * tq
    ntile = q0 // kt + 1

    w_t = jnp.transpose(ikw_ref[pl.ds(pl.multiple_of(q0, tq), tq), :])
    qidx = q0 + lax.broadcasted_iota(I32, (kt, tq), 1)
    gmax_ref[...] = jnp.full(gmax_ref.shape, -jnp.inf, F32)

    def score_tile(t, carry):
        r0 = pl.multiple_of(t * kt, kt)
        ik = ikw_ref[pl.ds(r0, kt), :].astype(BF16)
        acc = jnp.zeros((kt, tq), F32)
        for h in range(N_IDX_HEADS):
            qh = iq_ref[:, h * LANES:(h + 1) * LANES]
            s = lax.dot_general(ik, qh, (((1,), (1,)), ((), ())), preferred_element_type=F32)
            acc = acc + jnp.maximum(s, 0.0) * w_t[IDX_DIM + h:IDX_DIM + h + 1, :]
        kidx = r0 + lax.broadcasted_iota(I32, (kt, tq), 0)
        sc = jnp.where(kidx <= qidx, acc, -jnp.inf)
        s_ref[pl.ds(r0, kt), :] = sc
        gmax_ref[...] = functools.reduce(
            jnp.maximum, [sc[g * k_top:(g + 1) * k_top] for g in range(kt // k_top)], gmax_ref[...])
        return carry

    lax.fori_loop(0, ntile, score_tile, 0)

    def count(pred):
        def body(t, cnt):
            parts = []
            for part in range(IDX_PARTS):
                r0 = pl.multiple_of(t * kt + part * pr, pr)
                idx = r0 + lax.broadcasted_iota(I32, (pr, tq), 0)
                hit = pred(s_ref[pl.ds(r0, pr), :], idx)
                parts.append(jnp.sum(hit.reshape(pr // SUBLANES, SUBLANES, tq), axis=0))
            return cnt + ((parts[0] + parts[1]) + (parts[2] + parts[3]))
        cnt = lax.fori_loop(0, ntile, body, jnp.zeros((SUBLANES, tq), I32))
        return jnp.sum(cnt, axis=0, keepdims=True)

    g = gmax_ref[...]
    g_lo = jnp.min(g, axis=0, keepdims=True)
    g_hi = jnp.max(g, axis=0, keepdims=True)
    span = jnp.maximum(jnp.abs(g_lo), jnp.abs(g_hi))
    lo_u = _f32_to_order(g_lo - span)
    hi_u = _f32_to_order(g_hi + span)
    p0 = jnp.min(lax.clz(lo_u ^ hi_u))
    keep = jnp.where(p0 > 0, lax.shift_left(jnp.int32(-1), (32 - p0) & 31), 0)

    def value_pass(p, thr_u, split):
        cand_u = thr_u | lax.shift_left(jnp.int32(1), 31 - p)
        cand = _order_to_f32(cand_u)
        cnt = count(lambda sc, idx: jnp.where(sc >= cand, 1, 0))
        below = jnp.where(cand_u >= 0, jnp.where(cand != cand, 1, 0), 0)
        thr_u = jnp.where(cnt + below * k_top >= k_top, cand_u, thr_u)
        return thr_u, jnp.maximum(split, jnp.where(cnt == k_top, 1, 0))

    thr_u, split = lax.fori_loop(
        p0, IDX_UNCHECKED_PASSES, lambda p, c: value_pass(p, *c),
        (lo_u & keep, jnp.zeros((1, tq), I32)))

    def checked_pass(carry):
        p, thr_u, split, _ = carry
        thr_u, split = value_pass(p, thr_u, split)
        return p + 1, thr_u, split, jnp.min(split)

    _, thr_u, _, _ = lax.while_loop(
        lambda c: jnp.logical_and(c[0] < 32, c[3] == 0), checked_pass,
        (jnp.maximum(p0, IDX_UNCHECKED_PASSES), thr_u, split, jnp.min(split)))
    thr = _order_to_f32(thr_u)
    cnt_gt = count(lambda sc, idx: jnp.where(sc > thr, 1, 0))
    cnt_eq = count(lambda sc, idx: jnp.where(sc == thr, 1, 0))
    need = k_top - cnt_gt
    finite = jnp.where(thr_u != NEG_INF_ORDER, 1, 0)
    tied = finite * jnp.where(cnt_eq > need, 1, 0)

    def tie_break():
        def index_pass(p, x):
            cand = x | lax.shift_left(jnp.int32(1), IDX_BITS - 1 - p)
            cnt = count(lambda sc, idx: jnp.where(sc == thr, jnp.where(idx < cand, 1, 0), 0))
            return jnp.where(cnt < need, cand, x)
        return lax.fori_loop(0, IDX_BITS, index_pass, jnp.zeros((1, tq), I32))

    last = lax.cond(jnp.max(tied) > 0, tie_break, lambda: jnp.full((1, tq), seq, I32))
    last = jnp.where(tied > 0, last, seq)
    eq_val = finite.astype(F32)

    def write_mask(t, carry):
        r0 = pl.multiple_of(t * kt, kt)
        sc = s_ref[pl.ds(r0, kt), :]
        idx = r0 + lax.broadcasted_iota(I32, (kt, tq), 0)
        tie_sel = jnp.where(idx <= last, eq_val, 0.0)
        sel = jnp.where(sc > thr, 1.0, jnp.where(sc == thr, tie_sel, 0.0))
        mask_ref[t] = jnp.transpose(sel).astype(mask_ref.dtype)
        return carry

    lax.fori_loop(0, ntile, write_mask, 0)

    def clear_rest(t, carry):
        mask_ref[t] = jnp.zeros((tq, kt), mask_ref.dtype)
        return carry

    lax.fori_loop(ntile, seq // kt, clear_rest, 0)


def _indexer_mask(ikw, proj_a, iq_col_block, seq, k_top):
    return pl.pallas_call(
        functools.partial(_indexer_kernel, seq=seq, k_top=k_top),
        grid=(seq // IDX_TQ,),
        in_specs=[pl.BlockSpec((seq, LANES), lambda b: (0, 0)),
                  pl.BlockSpec((IDX_TQ, N_IDX_HEADS * LANES), lambda b: (b, iq_col_block))],
        out_specs=pl.BlockSpec((seq // IDX_KT, IDX_TQ, IDX_KT), lambda b: (0, b, 0)),
        out_shape=jax.ShapeDtypeStruct((seq // IDX_KT, seq, IDX_KT), BF16),
        scratch_shapes=[pltpu.VMEM((seq, IDX_TQ), F32), pltpu.VMEM((k_top, IDX_TQ), F32)],
        compiler_params=_cparams(("parallel",), 48),
        name="indexer_topk_mask",
    )(ikw, proj_a)


ATT_TQ = 512
ATT_KT = 512


def _attn_kernel(q_ref, k_ref, v_ref, mask_ref, o_ref, m_ref, l_ref, acc_ref):
    i = pl.program_id(0)
    j = pl.program_id(1)

    @pl.when(j == 0)
    def _():
        m_ref[...] = jnp.full(m_ref.shape, NEG_BIG, F32)
        l_ref[...] = jnp.zeros(l_ref.shape, F32)
        acc_ref[...] = jnp.zeros(acc_ref.shape, F32)

    @pl.when(j <= i)
    def _():
        sel = mask_ref[0].astype(F32) > 0.5
        for h in range(N_HEADS_ATTN):
            cs = slice(h * HEAD_DIM, (h + 1) * HEAD_DIM)
            s = lax.dot_general(q_ref[:, cs], k_ref[:, cs], (((1,), (1,)), ((), ())),
                                preferred_element_type=F32)
            s = jnp.where(sel, s, NEG_BIG)
            m_prev = m_ref[h]
            m_new = jnp.maximum(m_prev, jnp.max(s, axis=1, keepdims=True))
            alpha = jnp.exp2(m_prev - m_new)
            p = jnp.exp2(s - jnp.concatenate([m_new] * (s.shape[1] // LANES), axis=1))
            l_ref[h] = alpha * l_ref[h] + jnp.sum(p, axis=1, keepdims=True)
            m_ref[h] = m_new
            acc_ref[:, cs] = alpha * acc_ref[:, cs] + jnp.dot(
                p.astype(BF16), v_ref[:, cs], preferred_element_type=F32)

    @pl.when(j == i)
    def _():
        for h in range(N_HEADS_ATTN):
            cs = slice(h * HEAD_DIM, (h + 1) * HEAD_DIM)
            o_ref[:, cs] = (acc_ref[:, cs] / l_ref[h]).astype(o_ref.dtype)


def _masked_attention(proj_qk, proj_v, mask, q_cb, k_cb, v_cb, seq):
    assert ATT_KT == IDX_KT and mask.shape == (seq // ATT_KT, seq, ATT_KT)
    tq, kt = ATT_TQ, ATT_KT
    return pl.pallas_call(
        _attn_kernel,
        grid=(seq // tq, seq // kt),
        in_specs=[pl.BlockSpec((tq, A_WIDTH), lambda i, j: (i, q_cb)),
                  pl.BlockSpec((kt, A_WIDTH), lambda i, j: (jnp.minimum(i, j), k_cb)),
                  pl.BlockSpec((kt, A_WIDTH), lambda i, j: (jnp.minimum(i, j), v_cb)),
                  pl.BlockSpec((1, tq, kt), lambda i, j: (jnp.minimum(i, j), i, 0))],
        out_specs=pl.BlockSpec((tq, A_WIDTH), lambda i, j: (i, 0)),
        out_shape=jax.ShapeDtypeStruct((seq, A_WIDTH), BF16),
        scratch_shapes=[pltpu.VMEM((N_HEADS_ATTN, tq, LANES), F32),
                        pltpu.VMEM((N_HEADS_ATTN, tq, LANES), F32),
                        pltpu.VMEM((tq, A_WIDTH), F32)],
        compiler_params=_cparams(("parallel", "arbitrary"), 40),
        name="masked_attention",
    )(proj_qk, proj_qk, proj_v, mask)


def _retention_kernel(q_ref, k_ref, v_ref, g_ref, idec_ref, qdec_ref, kdec_ref, cdec_ref,
                      o_ref, state_ref):
    @pl.when(pl.program_id(0) == 0)
    def _():
        state_ref[...] = jnp.zeros(state_ref.shape, F32)

    for h in range(N_RET_HEADS):
        ck = slice(h * RET_QK_DIM, (h + 1) * RET_QK_DIM)
        cv = slice(h * RET_V_DIM, (h + 1) * RET_V_DIM)
        q = q_ref[:, ck]
        k = k_ref[:, ck]
        v = v_ref[:, cv]
        scores = lax.dot_general(q, k, (((1,), (1,)), ((), ())),
                                 preferred_element_type=F32) * idec_ref[h]
        inner = jnp.dot(scores.astype(BF16), v, preferred_element_type=F32)
        state = state_ref[h]
        cross = jnp.dot(q, state.astype(BF16), preferred_element_type=F32) * qdec_ref[h]
        y = inner + cross
        mu = jnp.mean(y, axis=-1, keepdims=True)
        var = jnp.mean(jnp.square(y - mu), axis=-1, keepdims=True)
        yn = (y - mu) * lax.rsqrt(var + EPS)
        g = g_ref[:, cv]
        o_ref[:, cv] = (g * jax.nn.sigmoid(g) * yn).astype(o_ref.dtype)
        k_dec_t = jnp.transpose(k.astype(F32) * kdec_ref[h]).astype(BF16)
        state_ref[h] = state * cdec_ref[h] + jnp.dot(k_dec_t, v, preferred_element_type=F32)


def _retention(proj_qk, proj_v, proj_g, q_cb, k_cb, v_cb, g_cb, seq):
    c = RET_CHUNK
    h = N_RET_HEADS
    log_gamma = jnp.log1p(-jnp.exp2(-5.0 - jnp.arange(h, dtype=F32)))
    i = jnp.arange(c, dtype=F32)
    diff = i[:, None] - i[None, :]
    idec = jnp.where(diff[None] >= 0,
                     jnp.exp(jnp.maximum(diff, 0.0)[None] * log_gamma[:, None, None]), 0.0)
    kdec = jnp.exp((c - 1 - i)[None, :] * log_gamma[:, None])
    qdec = jnp.exp((i + 1)[None, :] * log_gamma[:, None])
    cdec = jnp.exp(c * log_gamma)
    qdec_b = jnp.broadcast_to(qdec[:, :, None], (h, c, RET_V_DIM))
    kdec_b = jnp.broadcast_to(kdec[:, :, None], (h, c, RET_QK_DIM))
    cdec_b = jnp.broadcast_to(cdec[:, None, None], (h, RET_QK_DIM, RET_V_DIM))
    const = lambda shape: pl.BlockSpec(shape, lambda n: (0, 0, 0))
    return pl.pallas_call(
        _retention_kernel,
        grid=(seq // c,),
        in_specs=[pl.BlockSpec((c, RET_QK_WIDTH), lambda n: (n, q_cb)),
                  pl.BlockSpec((c, RET_QK_WIDTH), lambda n: (n, k_cb)),
                  pl.BlockSpec((c, RET_V_WIDTH), lambda n: (n, v_cb)),
                  pl.BlockSpec((c, RET_V_WIDTH), lambda n: (n, g_cb)),
                  const((h, c, c)), const((h, c, RET_V_DIM)), const((h, c, RET_QK_DIM)),
                  const((h, RET_QK_DIM, RET_V_DIM))],
        out_specs=pl.BlockSpec((c, RET_V_WIDTH), lambda n: (n, 0)),
        out_shape=jax.ShapeDtypeStruct((seq, RET_V_WIDTH), BF16),
        scratch_shapes=[pltpu.VMEM((h, RET_QK_DIM, RET_V_DIM), F32)],
        compiler_params=_cparams(("arbitrary",), 24),
        name="retention",
    )(proj_qk, proj_qk, proj_v, proj_g, idec, qdec_b, kdec_b, cdec_b)


def _mem_attn_kernel(q_ref, kv_ref, o_ref):
    for h in range(N_MEM_HEADS):
        cs = slice(h * MEM_HEAD_DIM, (h + 1) * MEM_HEAD_DIM)
        vs = slice(MEM_WIDTH + h * MEM_HEAD_DIM, MEM_WIDTH + (h + 1) * MEM_HEAD_DIM)
        s = lax.dot_general(q_ref[:, cs], kv_ref[:, cs], (((1,), (1,)), ((), ())),
                            preferred_element_type=F32)
        e = jnp.exp(s - jnp.max(s, axis=-1, keepdims=True))
        p = e / jnp.sum(e, axis=-1, keepdims=True)
        o_ref[:, cs] = jnp.dot(p.astype(BF16), kv_ref[:, vs],
                               preferred_element_type=F32).astype(o_ref.dtype)


def _mem_attention(proj_a, mem_kv, q_cb, seq, tm):
    n_mem = mem_kv.shape[0]
    return pl.pallas_call(
        _mem_attn_kernel,
        grid=(seq // tm,),
        in_specs=[pl.BlockSpec((tm, MEM_WIDTH), lambda i: (i, q_cb)),
                  pl.BlockSpec((n_mem, 2 * MEM_WIDTH), lambda i: (0, 0))],
        out_specs=pl.BlockSpec((tm, MEM_WIDTH), lambda i: (i, 0)),
        out_shape=jax.ShapeDtypeStruct((seq, MEM_WIDTH), BF16),
        compiler_params=_cparams(("parallel",), 24),
        name="memory_attention",
    )(proj_a, mem_kv)


def _merge_kernel(oa_ref, ob_ref, oc_ref, wa_ref, wb_ref, wc_ref, g0_ref, g1_ref, g2_ref, o_ref):
    for c in range(0, o_ref.shape[1], PROJ_CHUNK):
        cs = slice(c, c + PROJ_CHUNK)
        mixed = g0_ref[:, cs] * jnp.dot(oa_ref[...], wa_ref[:, cs], preferred_element_type=F32)
        mixed = mixed + g1_ref[:, cs] * jnp.dot(ob_ref[...], wb_ref[:, cs], preferred_element_type=F32)
        mixed = mixed + g2_ref[:, cs] * jnp.dot(oc_ref[...], wc_ref[:, cs], preferred_element_type=F32)
        o_ref[:, cs] = mixed.astype(o_ref.dtype)


def _merge(o_a, o_b, o_c, w_a, w_b, w_c, proj_b, gate_cb0, tm, tn):
    seq = o_a.shape[0]
    d = w_a.shape[1]
    per_gate = d // tn
    act = lambda w: pl.BlockSpec((tm, w), lambda i, j: (i, 0))
    wgt = lambda k: pl.BlockSpec((k, tn), lambda i, j: (0, j))
    gate = lambda g: pl.BlockSpec((tm, tn), lambda i, j: (i, gate_cb0 + g * per_gate + j))
    return pl.pallas_call(
        _merge_kernel,
        grid=(seq // tm, d // tn),
        in_specs=[act(o_a.shape[1]), act(o_b.shape[1]), act(o_c.shape[1]),
                  wgt(w_a.shape[0]), wgt(w_b.shape[0]), wgt(w_c.shape[0]),
                  gate(0), gate(1), gate(2)],
        out_specs=pl.BlockSpec((tm, tn), lambda i, j: (i, j)),
        out_shape=jax.ShapeDtypeStruct((seq, d), BF16),
        compiler_params=_cparams(("parallel", "arbitrary"), 48),
        name="gated_merge",
    )(o_a, o_b, o_c, w_a, w_b, w_c, proj_b, proj_b, proj_b)


def _mm_norm_res_kernel(a_ref, w_ref, x_ref, g_ref, *rest, with_next_norm):
    if with_next_norm:
        g2_ref, o_ref, h_ref = rest
    else:
        (o_ref,) = rest
    d = w_ref.shape[1]
    cols = [slice(c, c + PROJ_CHUNK) for c in range(0, d, PROJ_CHUNK)]
    accs = [jnp.dot(a_ref[...], w_ref[:, cs], preferred_element_type=F32) for cs in cols]
    ssq = sum(jnp.sum(acc * acc, axis=-1, keepdims=True) for acc in accs)
    inv = lax.rsqrt(ssq / d + EPS)
    x_new = [x_ref[:, cs] + acc * inv * g_ref[:, cs] for cs, acc in zip(cols, accs)]
    for cs, xc in zip(cols, x_new):
        o_ref[:, cs] = xc
    if with_next_norm:
        ssq2 = sum(jnp.sum(xc * xc, axis=-1, keepdims=True) for xc in x_new)
        inv2 = lax.rsqrt(ssq2 / d + EPS)
        for cs, xc in zip(cols, x_new):
            h_ref[:, cs] = (xc * inv2 * g2_ref[:, cs]).astype(h_ref.dtype)


def _mm_norm_res(a, w, x, g, g2, tm, vmem_mb, name):
    m, k = a.shape
    d = w.shape[1]
    row = lambda width: pl.BlockSpec((tm, width), lambda i: (i, 0))
    vec = pl.BlockSpec((1, d), lambda i: (0, 0))
    with_next_norm = g2 is not None
    in_specs = [row(k), pl.BlockSpec((k, d), lambda i: (0, 0), pipeline_mode=pl.Buffered(1)),
                row(d), vec]
    args = [a, w, x, g.reshape(1, d)]
    out_specs = [row(d)]
    out_shape = [jax.ShapeDtypeStruct((m, d), F32)]
    if with_next_norm:
        in_specs.append(vec)
        args.append(g2.reshape(1, d))
        out_specs.append(row(d))
        out_shape.append(jax.ShapeDtypeStruct((m, d), BF16))
    return pl.pallas_call(
        functools.partial(_mm_norm_res_kernel, with_next_norm=with_next_norm),
        grid=(m // tm,),
        in_specs=in_specs,
        out_specs=out_specs,
        out_shape=out_shape,
        compiler_params=_cparams(("parallel",), vmem_mb),
        name=name,
    )(*args)


def _swiglu_kernel(a_ref, wg_ref, wu_ref, o_ref, wg_bf_ref, wu_bf_ref):
    @pl.when(pl.program_id(1) == 0)
    def _():
        wg_bf_ref[...] = wg_ref[...].astype(BF16)
        wu_bf_ref[...] = wu_ref[...].astype(BF16)

    gate = jnp.dot(a_ref[...], wg_bf_ref[...], preferred_element_type=F32)
    up = jnp.dot(a_ref[...], wu_bf_ref[...], preferred_element_type=F32)
    o_ref[...] = (gate * jax.nn.sigmoid(gate) * up).astype(o_ref.dtype)


def _swiglu(a, w, tm, tn):
    m, k = a.shape
    d_ff = w.shape[1] // 2
    nt = d_ff // tn
    return pl.pallas_call(
        _swiglu_kernel,
        grid=(nt, m // tm),
        in_specs=[pl.BlockSpec((tm, k), lambda j, i: (i, 0)),
                  pl.BlockSpec((k, tn), lambda j, i: (0, j)),
                  pl.BlockSpec((k, tn), lambda j, i: (0, nt + j))],
        out_specs=pl.BlockSpec((tm, tn), lambda j, i: (i, j)),
        out_shape=jax.ShapeDtypeStruct((m, d_ff), BF16),
        scratch_shapes=[pltpu.VMEM((k, tn), BF16), pltpu.VMEM((k, tn), BF16)],
        compiler_params=_cparams(("arbitrary", "arbitrary"), 48),
        name="swiglu_in",
    )(a, w, w)


def _layer(x, mem, positions, g_pre_mix, g_mem, w_in, w_mem_kv, w_branch_a, w_branch_b,
           w_branch_c, w_out, g_post_mix, g_pre_ffn, w_ffn_in, w_ffn_out, g_post_ffn):
    seq, d = x.shape
    k_top = min(TOPK_MAX, seq // 4)

    sizes = (A_WIDTH, A_WIDTH, A_WIDTH, IDX_Q_WIDTH, IDX_DIM, N_IDX_HEADS,
             RET_QK_WIDTH, RET_QK_WIDTH, RET_V_WIDTH, RET_V_WIDTH, MEM_WIDTH, 3 * d)
    (o_aq, o_ak, o_av, o_iq, o_ik, o_iw, o_rq, o_rk, o_rv, o_rg, o_mq, o_gate) = (
        int(o) for o in np.cumsum((0,) + sizes[:-1]))
    w_in_t = jnp.swapaxes(w_in, 0, 1)

    tabs = _rope_tables(positions, 1024)
    (c128, s128, c64, sa64, sb64, ck, sak, sbk) = tabs
    t128 = (c128, s128, s128)
    t64 = (c64, sa64, sb64)

    h = _rmsnorm(x, g_pre_mix, 512)

    tn = 1024
    proj_iq = _proj(h, w_in_t, [o_iq, o_iq + tn // 2], t64, ROPE64, [IDX_DIM ** -0.5] * 2, BF16,
                    1024, tn, 48, "proj_index_q", pad_heads=True)
    proj_r = _proj(h, w_in_t, [o_aq, o_ak, o_rq, o_rk], t128, ROPE128,
                   [HEAD_DIM ** -0.5 * LOG2E, 1.0, 1.0, RET_QK_DIM ** -0.5], BF16, 1024, tn, 48,
                   "proj_rope")
    proj_p = _proj(h, w_in_t, [o_rv, o_rv + tn, o_av, o_mq], t128, PLAIN,
                   [1.0, 1.0, 1.0, MEM_HEAD_DIM ** -0.5], BF16, 1024, tn, 48, "proj_plain")
    gates = _proj(h, w_in_t, [o_gate + t * tn for t in range(3 * d // tn)], t128, SIGMOID,
                  [1.0] * (3 * d // tn), F32, 1024, tn, 52, "proj_gates")
    r_g = _proj(h, w_in_t, [o_rg, o_rg + tn], t128, PLAIN, [1.0] * 2, F32, 1024, tn, 52,
                "proj_ret_gate")
    assert o_iw == o_ik + IDX_DIM
    ikw = _proj(h, w_in_t, [o_ik], (ck, sak, sbk), ROPE64, [1.0], F32, 1024, LANES, 24,
                "proj_index_kw", valid_rows=IDX_DIM + N_IDX_HEADS)

    mask = _indexer_mask(ikw, proj_iq, 0, seq, k_top)
    o_a = _masked_attention(proj_r, proj_p, mask, 0, 1, 2, seq)

    o_b = _retention(proj_r, proj_p, r_g, 2, 3, 0, 0, seq)

    mem_n = _rmsnorm(mem, g_mem, mem.shape[0])
    mem_kv = _proj(mem_n, jnp.swapaxes(w_mem_kv, 0, 1), [0, tn], t128, PLAIN, [1.0] * 2, BF16,
                   mem.shape[0], tn, 40, "proj_mem_kv")
    o_c = _mem_attention(proj_p, mem_kv, 3, seq, 512)

    mixed = _merge(o_a, o_b, o_c, w_branch_a.astype(BF16), w_branch_b.astype(BF16),
                   w_branch_c.astype(BF16), gates, 0, 512, tn)
    x1, h2 = _mm_norm_res(mixed, w_out.astype(BF16), x, g_post_mix, g_pre_ffn, 512, 40,
                          "out_proj_norm_res")

    act = _swiglu(h2, w_ffn_in, 1024, 512)
    (x2,) = _mm_norm_res(act, w_ffn_out.astype(BF16), x1, g_post_ffn, None, 256, 48,
                         "ffn_out_norm_res")
    return x2


def kernel(x, mem, positions, g_pre_mix, g_mem, w_in, w_mem_kv, w_branch_a, w_branch_b, w_branch_c,
           w_out, g_post_mix, g_pre_ffn, w_ffn_in, w_ffn_out, g_post_ffn):
    assert x.shape[0] == 1 and g_pre_mix.shape[0] == 1
    out = _layer(x[0], mem[0], positions[0], g_pre_mix[0], g_mem[0], w_in[0], w_mem_kv[0],
                 w_branch_a[0], w_branch_b[0], w_branch_c[0], w_out[0], g_post_mix[0],
                 g_pre_ffn[0], w_ffn_in[0], w_ffn_out[0], g_post_ffn[0])
    return out[None]
```

```python
import functools

import jax
import jax.numpy as jnp
import numpy as np
from jax import lax
from jax.experimental import pallas as pl
from jax.experimental.pallas import tpu as pltpu

D_MODEL = 2048
N_HEADS_ATTN = 8
HEAD_DIM = 128
N_IDX_HEADS = 16
IDX_DIM = 64
TOPK_MAX = 256
N_RET_HEADS = 8
RET_QK_DIM = 128
RET_V_DIM = 256
RET_CHUNK = 128
N_MEM_HEADS = 4
MEM_HEAD_DIM = 256
ROPE_THETA = 10000.0
EPS = 1e-6

A_WIDTH = N_HEADS_ATTN * HEAD_DIM
IDX_Q_WIDTH = N_IDX_HEADS * IDX_DIM
RET_QK_WIDTH = N_RET_HEADS * RET_QK_DIM
RET_V_WIDTH = N_RET_HEADS * RET_V_DIM
MEM_WIDTH = N_MEM_HEADS * MEM_HEAD_DIM

LANES = 128
SUBLANES = 8
VMEM_BYTES_V7X = 64 * 1024 * 1024
INT_MIN = -(2 ** 31)
NEG_BIG = -1e30
LOG2E = 1.4426950408889634

F32 = jnp.float32
BF16 = jnp.bfloat16
I32 = jnp.int32

PLAIN, ROPE128, ROPE64, SIGMOID = 0, 1, 2, 3


def _cparams(sem, vmem_mb):
    assert vmem_mb * 1024 * 1024 < VMEM_BYTES_V7X
    return pltpu.CompilerParams(dimension_semantics=sem, vmem_limit_bytes=vmem_mb * 1024 * 1024)


def _rms(x, g):
    return x * lax.rsqrt(jnp.mean(x * x, axis=-1, keepdims=True) + EPS) * g


def _rmsnorm_kernel(x_ref, g_ref, o_ref):
    o_ref[...] = _rms(x_ref[...], g_ref[...]).astype(o_ref.dtype)


def _rmsnorm(x, g, tm):
    n, d = x.shape
    return pl.pallas_call(
        _rmsnorm_kernel,
        grid=(n // tm,),
        in_specs=[pl.BlockSpec((tm, d), lambda i: (i, 0)), pl.BlockSpec((1, d), lambda i: (0, 0))],
        out_specs=pl.BlockSpec((tm, d), lambda i: (i, 0)),
        out_shape=jax.ShapeDtypeStruct((n, d), BF16),
        compiler_params=_cparams(("parallel",), 24),
        name="rmsnorm",
    )(x, g.reshape(1, d))


def _rope_table_kernel(pos_ref, c_ref, c128_ref, s128_ref, c64_ref, sa64_ref, sb64_ref,
                       ck_ref, sak_ref, sbk_ref):
    pos = pos_ref[...]
    ang128 = pos * c_ref[0:1, :]
    ang64 = pos * c_ref[1:2, :]
    c128_ref[...] = jnp.cos(ang128)
    s128_ref[...] = jnp.sin(ang128) * c_ref[2:3, :]
    cos64 = jnp.cos(ang64)
    sin64 = jnp.sin(ang64)
    sa = sin64 * c_ref[3:4, :]
    sb = sin64 * c_ref[4:5, :]
    c64_ref[...] = cos64
    sa64_ref[...] = sa
    sb64_ref[...] = sb
    ck_ref[...] = cos64 * c_ref[5:6, :] + c_ref[6:7, :]
    sak_ref[...] = sa * c_ref[5:6, :]
    sbk_ref[...] = sb * c_ref[5:6, :]


def _rope_tables(positions, tm):
    s = positions.shape[-1]
    pos_b = jnp.broadcast_to(positions.reshape(s, 1).astype(F32), (s, LANES))
    lane = np.arange(LANES)
    f128 = ROPE_THETA ** (-jnp.arange(0, HEAD_DIM, 2, dtype=F32) / HEAD_DIM)
    f64 = ROPE_THETA ** (-jnp.arange(0, IDX_DIM, 2, dtype=F32) / IDX_DIM)
    half64 = (lane % IDX_DIM) < IDX_DIM // 2
    rows = [
        jnp.tile(f128, 2),
        jnp.tile(f64, 4),
        jnp.asarray(np.where(lane < HEAD_DIM // 2, -1.0, 1.0), F32),
        jnp.asarray(np.where(half64, -1.0, 0.0), F32),
        jnp.asarray(np.where(half64, 0.0, 1.0), F32),
        jnp.asarray(np.where(lane < IDX_DIM, 1.0, 0.0), F32),
        jnp.asarray(np.where(lane < IDX_DIM, 0.0,
                             np.where(lane < IDX_DIM + N_IDX_HEADS, N_IDX_HEADS ** -0.5, 1.0)), F32),
        jnp.zeros((LANES,), F32),
    ]
    consts = jnp.stack(rows)
    spec = pl.BlockSpec((tm, LANES), lambda i: (i, 0))
    return pl.pallas_call(
        _rope_table_kernel,
        grid=(s // tm,),
        in_specs=[spec, pl.BlockSpec((SUBLANES, LANES), lambda i: (0, 0))],
        out_specs=[spec] * 8,
        out_shape=[jax.ShapeDtypeStruct((s, LANES), F32)] * 8,
        compiler_params=_cparams(("parallel",), 24),
        name="rope_tables",
    )(pos_b, consts)


def _rope128(x, cos, sin_signed):
    return x * cos + pltpu.roll(x, HEAD_DIM // 2, 1) * sin_signed


def _rope64(x, cos, sin_a, sin_b):
    return (x * cos + pltpu.roll(x, LANES - IDX_DIM // 2, 1) * sin_a
            + pltpu.roll(x, IDX_DIM // 2, 1) * sin_b)


PROJ_CHUNK = 2 * LANES


def _select_by_tile(j, values, dtype):
    out = jnp.asarray(values[-1], dtype)
    for t in range(len(values) - 2, -1, -1):
        out = jnp.where(j == t, jnp.asarray(values[t], dtype), out)
    return out


def _proj_kernel(a_ref, wt_ref, t0_ref, t1_ref, t2_ref, o_ref, wbf_ref, *, mode, scales, tn,
                 valid_rows, pad_heads):
    j = pl.program_id(0)

    @pl.when(pl.program_id(1) == 0)
    def _():
        if pad_heads:
            zeros = jnp.zeros((LANES - IDX_DIM, wbf_ref.shape[1]), BF16)
            for h in range(tn // LANES):
                wbf_ref[h * LANES:h * LANES + IDX_DIM, :] = (
                    wt_ref[h * IDX_DIM:(h + 1) * IDX_DIM, :].astype(BF16))
                wbf_ref[h * LANES + IDX_DIM:(h + 1) * LANES, :] = zeros
        else:
            w = wt_ref[...]
            if valid_rows < tn:
                w = jnp.where(lax.broadcasted_iota(I32, w.shape, 0) < valid_rows, w, 0.0)
            wbf_ref[...] = w.astype(BF16)

    scale = _select_by_tile(j, scales, F32)
    uniform = all(s == scales[0] for s in scales)
    cw = min(PROJ_CHUNK, tn)
    for c in range(tn // cw):
        acc = lax.dot_general(a_ref[...], wbf_ref[c * cw:(c + 1) * cw, :], (((1,), (1,)), ((), ())),
                              preferred_element_type=F32)
        for s in range(cw // LANES):
            x = acc[:, s * LANES:(s + 1) * LANES]
            if mode == ROPE128:
                x = _rope128(x, t0_ref[...], t1_ref[...])
            elif mode == ROPE64:
                x = _rope64(x, t0_ref[...], t1_ref[...], t2_ref[...])
            elif mode == SIGMOID:
                x = jax.nn.sigmoid(x)
            if not (uniform and scales[0] == 1.0):
                x = x * scale
            col = c * cw + s * LANES
            o_ref[:, col:col + LANES] = x.astype(o_ref.dtype)


def _proj(a, wt, row_offsets, tables, mode, scales, out_dtype, tm, tn, vmem_mb, name, *,
          valid_rows=None, pad_heads=False):
    m, k = a.shape
    assert wt.shape[1] == k and len(row_offsets) == len(scales) and tn % min(PROJ_CHUNK, tn) == 0
    src_rows = tn // 2 if pad_heads else tn
    valid_rows = tn if valid_rows is None else valid_rows
    assert all(o % SUBLANES == 0 and o + src_rows <= wt.shape[0] for o in row_offsets)
    tspec = pl.BlockSpec((tm, LANES), lambda j, i: (i, 0))
    return pl.pallas_call(
        functools.partial(_proj_kernel, mode=mode, scales=tuple(scales), tn=tn,
                          valid_rows=valid_rows, pad_heads=pad_heads),
        grid=(len(row_offsets), m // tm),
        in_specs=[pl.BlockSpec((tm, k), lambda j, i: (i, 0)),
                  pl.BlockSpec((pl.Element(src_rows), pl.Element(k)),
                               lambda j, i: (_select_by_tile(
                                   j, [o // SUBLANES for o in row_offsets], I32) * SUBLANES, 0)),
                  tspec, tspec, tspec],
        out_specs=pl.BlockSpec((tm, tn), lambda j, i: (i, j)),
        out_shape=jax.ShapeDtypeStruct((m, len(row_offsets) * tn), out_dtype),
        scratch_shapes=[pltpu.VMEM((tn, k), BF16)],
        compiler_params=_cparams(("arbitrary", "arbitrary"), vmem_mb),
        name=name,
    )(a, wt, *tables)


IDX_TQ = 256
IDX_KT = 512
IDX_PARTS = 4
IDX_BITS = 13
IDX_UNCHECKED_PASSES = 12
IDX_BRACKET_SLACK = 1.0 / 16


def _f32_key(x):
    bits = lax.bitcast_convert_type(x, I32)
    return bits ^ (lax.shift_right_arithmetic(bits, 31) & jnp.int32(0x7FFFFFFF))


def _key_f32(key):
    bits = key ^ (lax.shift_right_arithmetic(key, 31) & jnp.int32(0x7FFFFFFF))
    return lax.bitcast_convert_type(bits, F32)


def _indexer_kernel(ikw_ref, iq_ref, mask_ref, s_ref, gmax_ref, *, seq, k_top):
    b = pl.program_id(0)
    tq, kt = IDX_TQ, IDX_KT
    pr = kt // IDX_PARTS
    q0 = b * tq
    ntile = q0 // kt + 1

    w_t = jnp.transpose(ikw_ref[pl.ds(pl.multiple_of(q0, tq), tq), :])
    qidx = q0 + lax.broadcasted_iota(I32, (kt, tq), 1)
    gmax_ref[...] = jnp.full(gmax_ref.shape, -jnp.inf, F32)

    def score_tile(t, carry):
        r0 = pl.multiple_of(t * kt, kt)
        ik = ikw_ref[pl.ds(r0, kt), :].astype(BF16)
        acc = jnp.zeros((kt, tq), F32)
        for h in range(N_IDX_HEADS):
            qh = iq_ref[:, h * LANES:(h + 1) * LANES]
            s = lax.dot_general(ik, qh, (((1,), (1,)), ((), ())), preferred_element_type=F32)
            acc = acc + jnp.maximum(s, 0.0) * w_t[IDX_DIM + h:IDX_DIM + h + 1, :]
        kidx = r0 + lax.broadcasted_iota(I32, (kt, tq), 0)
        sc = jnp.where(kidx <= qidx, acc, -jnp.inf)
        s_ref[pl.ds(r0, kt), :] = sc
        gmax_ref[...] = functools.reduce(
            jnp.maximum, [sc[g * k_top:(g + 1) * k_top] for g in range(kt // k_top)], gmax_ref[...])
        return carry

    lax.fori_loop(0, ntile, score_tile, 0)

    def count(pred):
        def body(t, cnt):
            parts = []
            for part in range(IDX_PARTS):
                r0 = pl.multiple_of(t * kt + part * pr, pr)
                idx = r0 + lax.broadcasted_iota(I32, (pr, tq), 0)
                hit = pred(s_ref[pl.ds(r0, pr), :], idx)
                parts.append(jnp.sum(hit.reshape(pr // SUBLANES, SUBLANES, tq), axis=0))
            return cnt + ((parts[0] + parts[1]) + (parts[2] + parts[3]))
        cnt = lax.fori_loop(0, ntile, body, jnp.zeros((SUBLANES, tq), I32))
        return jnp.sum(cnt, axis=0, keepdims=True)

    g = gmax_ref[...]
    g_lo = jnp.min(g, axis=0, keepdims=True)
    g_hi = jnp.max(g, axis=0, keepdims=True)
    slack = jnp.maximum(jnp.abs(g_lo), jnp.abs(g_hi)) * IDX_BRACKET_SLACK
    lo0 = _f32_key(g_lo - slack)
    hi0 = _f32_key(g_hi + slack)

    def search_pass(lo, hi, split):
        mid = (lo | hi) - lax.shift_right_arithmetic(lo ^ hi, 1)
        cand = _key_f32(mid)
        cnt = count(lambda sc, idx: jnp.where(sc >= cand, 1, 0))
        exact = cnt == k_top
        enough = cnt >= k_top
        lo = jnp.where(enough, mid, lo)
        hi = jnp.where(exact, mid, jnp.where(enough, hi, mid - 1))
        return lo, hi, jnp.maximum(split, jnp.where(exact, 1, 0))

    state = lax.fori_loop(0, IDX_UNCHECKED_PASSES, lambda p, c: search_pass(*c),
                          (lo0, hi0, jnp.zeros((1, tq), I32)))

    def checked_passes(carry):
        lo, hi, split = search_pass(*search_pass(*carry[:3]))
        return lo, hi, split, jnp.max(jnp.where(lo == hi, 0, 1))

    lo, hi, split, _ = lax.while_loop(
        lambda c: c[3] > 0, checked_passes,
        (*state, jnp.max(jnp.where(state[0] == state[1], 0, 1))))
    thr = _key_f32(lo)

    def write_mask_tiles(select):
        def body(t, carry):
            r0 = pl.multiple_of(t * kt, kt)
            idx = r0 + lax.broadcasted_iota(I32, (kt, tq), 0)
            sel = select(s_ref[pl.ds(r0, kt), :], idx)
            mask_ref[t] = jnp.transpose(sel).astype(mask_ref.dtype)
            return carry
        lax.fori_loop(0, ntile, body, 0)

    def separated():
        write_mask_tiles(lambda sc, idx: jnp.where(sc >= thr, 1.0, 0.0))

    def with_ties():
        cnt_gt = count(lambda sc, idx: jnp.where(sc > thr, 1, 0))
        cnt_eq = count(lambda sc, idx: jnp.where(sc == thr, 1, 0))
        need = k_top - cnt_gt
        finite = jnp.where(thr > -jnp.inf, 1, 0)
        tied = finite * jnp.where(cnt_eq > need, 1, 0)

        def tie_break():
            def index_pass(p, x):
                cand = x | lax.shift_left(jnp.int32(1), IDX_BITS - 1 - p)
                cnt = count(lambda sc, idx: jnp.where(sc == thr, jnp.where(idx < cand, 1, 0), 0))
                return jnp.where(cnt < need, cand, x)
            return lax.fori_loop(0, IDX_BITS, index_pass, jnp.zeros((1, tq), I32))

        last = lax.cond(jnp.max(tied) > 0, tie_break, lambda: jnp.full((1, tq), seq, I32))
        last = jnp.where(tied > 0, last, seq)
        eq_val = finite.astype(F32)
        write_mask_tiles(lambda sc, idx: jnp.where(
            sc > thr, 1.0, jnp.where(sc == thr, jnp.where(idx <= last, eq_val, 0.0), 0.0)))

    lax.cond(jnp.min(split) > 0, separated, with_ties)

    def clear_rest(t, carry):
        mask_ref[t] = jnp.zeros((tq, kt), mask_ref.dtype)
        return carry

    lax.fori_loop(ntile, seq // kt, clear_rest, 0)


def _indexer_mask(ikw, proj_a, iq_col_block, seq, k_top):
    return pl.pallas_call(
        functools.partial(_indexer_kernel, seq=seq, k_top=k_top),
        grid=(seq // IDX_TQ,),
        in_specs=[pl.BlockSpec((seq, LANES), lambda b: (0, 0)),
                  pl.BlockSpec((IDX_TQ, N_IDX_HEADS * LANES), lambda b: (b, iq_col_block))],
        out_specs=pl.BlockSpec((seq // IDX_KT, IDX_TQ, IDX_KT), lambda b: (0, b, 0)),
        out_shape=jax.ShapeDtypeStruct((seq // IDX_KT, seq, IDX_KT), BF16),
        scratch_shapes=[pltpu.VMEM((seq, IDX_TQ), F32), pltpu.VMEM((k_top, IDX_TQ), F32)],
        compiler_params=_cparams(("parallel",), 48),
        name="indexer_topk_mask",
    )(ikw, proj_a)


ATT_TQ = 512
ATT_KT = 512


def _attn_kernel(q_ref, k_ref, v_ref, mask_ref, o_ref, m_ref, l_ref, acc_ref):
    i = pl.program_id(0)
    j = pl.program_id(1)

    @pl.when(j == 0)
    def _():
        m_ref[...] = jnp.full(m_ref.shape, NEG_BIG, F32)
        l_ref[...] = jnp.zeros(l_ref.shape, F32)
        acc_ref[...] = jnp.zeros(acc_ref.shape, F32)

    @pl.when(j <= i)
    def _():
        sel = mask_ref[0].astype(F32) > 0.5
        for h in range(N_HEADS_ATTN):
            cs = slice(h * HEAD_DIM, (h + 1) * HEAD_DIM)
            s = lax.dot_general(q_ref[:, cs], k_ref[:, cs], (((1,), (1,)), ((), ())),
                                preferred_element_type=F32)
            s = jnp.where(sel, s, NEG_BIG)
            m_prev = m_ref[h]
            m_new = jnp.maximum(m_prev, jnp.max(s, axis=1, keepdims=True))
            alpha = jnp.exp2(m_prev - m_new)
            p = jnp.exp2(s - jnp.concatenate([m_new] * (s.shape[1] // LANES), axis=1))
            l_ref[h] = alpha * l_ref[h] + jnp.sum(p, axis=1, keepdims=True)
            m_ref[h] = m_new
            acc_ref[:, cs] = alpha * acc_ref[:, cs] + jnp.dot(
                p.astype(BF16), v_ref[:, cs], preferred_element_type=F32)

    @pl.when(j == i)
    def _():
        for h in range(N_HEADS_ATTN):
            cs = slice(h * HEAD_DIM, (h + 1) * HEAD_DIM)
            o_ref[:, cs] = (acc_ref[:, cs] / l_ref[h]).astype(o_ref.dtype)


def _masked_attention(proj_qk, proj_v, mask, q_cb, k_cb, v_cb, seq):
    assert ATT_KT == IDX_KT and mask.shape == (seq // ATT_KT, seq, ATT_KT)
    tq, kt = ATT_TQ, ATT_KT
    return pl.pallas_call(
        _attn_kernel,
        grid=(seq // tq, seq // kt),
        in_specs=[pl.BlockSpec((tq, A_WIDTH), lambda i, j: (i, q_cb)),
                  pl.BlockSpec((kt, A_WIDTH), lambda i, j: (jnp.minimum(i, j), k_cb)),
                  pl.BlockSpec((kt, A_WIDTH), lambda i, j: (jnp.minimum(i, j), v_cb)),
                  pl.BlockSpec((1, tq, kt), lambda i, j: (jnp.minimum(i, j), i, 0))],
        out_specs=pl.BlockSpec((tq, A_WIDTH), lambda i, j: (i, 0)),
        out_shape=jax.ShapeDtypeStruct((seq, A_WIDTH), BF16),
        scratch_shapes=[pltpu.VMEM((N_HEADS_ATTN, tq, LANES), F32),
                        pltpu.VMEM((N_HEADS_ATTN, tq, LANES), F32),
                        pltpu.VMEM((tq, A_WIDTH), F32)],
        compiler_params=_cparams(("parallel", "arbitrary"), 40),
        name="masked_attention",
    )(proj_qk, proj_qk, proj_v, mask)


def _retention_kernel(q_ref, k_ref, v_ref, g_ref, idec_ref, qdec_ref, kdec_ref, cdec_ref,
                      o_ref, state_ref):
    @pl.when(pl.program_id(0) == 0)
    def _():
        state_ref[...] = jnp.zeros(state_ref.shape, F32)

    for h in range(N_RET_HEADS):
        ck = slice(h * RET_QK_DIM, (h + 1) * RET_QK_DIM)
        cv = slice(h * RET_V_DIM, (h + 1) * RET_V_DIM)
        q = q_ref[:, ck]
        k = k_ref[:, ck]
        v = v_ref[:, cv]
        scores = lax.dot_general(q, k, (((1,), (1,)), ((), ())),
                                 preferred_element_type=F32) * idec_ref[h]
        inner = jnp.dot(scores.astype(BF16), v, preferred_element_type=F32)
        state = state_ref[h]
        cross = jnp.dot(q, state.astype(BF16), preferred_element_type=F32) * qdec_ref[h]
        y = inner + cross
        mu = jnp.mean(y, axis=-1, keepdims=True)
        var = jnp.mean(jnp.square(y - mu), axis=-1, keepdims=True)
        yn = (y - mu) * lax.rsqrt(var + EPS)
        g = g_ref[:, cv]
        o_ref[:, cv] = (g * jax.nn.sigmoid(g) * yn).astype(o_ref.dtype)
        k_dec_t = jnp.transpose(k.astype(F32) * kdec_ref[h]).astype(BF16)
        state_ref[h] = state * cdec_ref[h] + jnp.dot(k_dec_t, v, preferred_element_type=F32)


def _retention(proj_qk, proj_v, proj_g, q_cb, k_cb, v_cb, g_cb, seq):
    c = RET_CHUNK
    h = N_RET_HEADS
    log_gamma = jnp.log1p(-jnp.exp2(-5.0 - jnp.arange(h, dtype=F32)))
    i = jnp.arange(c, dtype=F32)
    diff = i[:, None] - i[None, :]
    idec = jnp.where(diff[None] >= 0,
                     jnp.exp(jnp.maximum(diff, 0.0)[None] * log_gamma[:, None, None]), 0.0)
    kdec = jnp.exp((c - 1 - i)[None, :] * log_gamma[:, None])
    qdec = jnp.exp((i + 1)[None, :] * log_gamma[:, None])
    cdec = jnp.exp(c * log_gamma)
    qdec_b = jnp.broadcast_to(qdec[:, :, None], (h, c, RET_V_DIM))
    kdec_b = jnp.broadcast_to(kdec[:, :, None], (h, c, RET_QK_DIM))
    cdec_b = jnp.broadcast_to(cdec[:, None, None], (h, RET_QK_DIM, RET_V_DIM))
    const = lambda shape: pl.BlockSpec(shape, lambda n: (0, 0, 0))
    return pl.pallas_call(
        _retention_kernel,
        grid=(seq // c,),
        in_specs=[pl.BlockSpec((c, RET_QK_WIDTH), lambda n: (n, q_cb)),
                  pl.BlockSpec((c, RET_QK_WIDTH), lambda n: (n, k_cb)),
                  pl.BlockSpec((c, RET_V_WIDTH), lambda n: (n, v_cb)),
                  pl.BlockSpec((c, RET_V_WIDTH), lambda n: (n, g_cb)),
                  const((h, c, c)), const((h, c, RET_V_DIM)), const((h, c, RET_QK_DIM)),
                  const((h, RET_QK_DIM, RET_V_DIM))],
        out_specs=pl.BlockSpec((c, RET_V_WIDTH), lambda n: (n, 0)),
        out_shape=jax.ShapeDtypeStruct((seq, RET_V_WIDTH), BF16),
        scratch_shapes=[pltpu.VMEM((h, RET_QK_DIM, RET_V_DIM), F32)],
        compiler_params=_cparams(("arbitrary",), 24),
        name="retention",
    )(proj_qk, proj_qk, proj_v, proj_g, idec, qdec_b, kdec_b, cdec_b)


def _mem_attn_kernel(q_ref, kv_ref, o_ref):
    for h in range(N_MEM_HEADS):
        cs = slice(h * MEM_HEAD_DIM, (h + 1) * MEM_HEAD_DIM)
        vs = slice(MEM_WIDTH + h * MEM_HEAD_DIM, MEM_WIDTH + (h + 1) * MEM_HEAD_DIM)
        s = lax.dot_general(q_ref[:, cs], kv_ref[:, cs], (((1,), (1,)), ((), ())),
                            preferred_element_type=F32)
        e = jnp.exp(s - jnp.max(s, axis=-1, keepdims=True))
        p = e / jnp.sum(e, axis=-1, keepdims=True)
        o_ref[:, cs] = jnp.dot(p.astype(BF16), kv_ref[:, vs],
                               preferred_element_type=F32).astype(o_ref.dtype)


def _mem_attention(proj_a, mem_kv, q_cb, seq, tm):
    n_mem = mem_kv.shape[0]
    return pl.pallas_call(
        _mem_attn_kernel,
        grid=(seq // tm,),
        in_specs=[pl.BlockSpec((tm, MEM_WIDTH), lambda i: (i, q_cb)),
                  pl.BlockSpec((n_mem, 2 * MEM_WIDTH), lambda i: (0, 0))],
        out_specs=pl.BlockSpec((tm, MEM_WIDTH), lambda i: (i, 0)),
        out_shape=jax.ShapeDtypeStruct((seq, MEM_WIDTH), BF16),
        compiler_params=_cparams(("parallel",), 24),
        name="memory_attention",
    )(proj_a, mem_kv)


def _merge_kernel(oa_ref, ob_ref, oc_ref, wa_ref, wb_ref, wc_ref, g0_ref, g1_ref, g2_ref, o_ref):
    for c in range(0, o_ref.shape[1], PROJ_CHUNK):
        cs = slice(c, c + PROJ_CHUNK)
        mixed = g0_ref[:, cs] * jnp.dot(oa_ref[...], wa_ref[:, cs], preferred_element_type=F32)
        mixed = mixed + g1_ref[:, cs] * jnp.dot(ob_ref[...], wb_ref[:, cs], preferred_element_type=F32)
        mixed = mixed + g2_ref[:, cs] * jnp.dot(oc_ref[...], wc_ref[:, cs], preferred_element_type=F32)
        o_ref[:, cs] = mixed.astype(o_ref.dtype)


def _merge(o_a, o_b, o_c, w_a, w_b, w_c, proj_b, gate_cb0, tm, tn):
    seq = o_a.shape[0]
    d = w_a.shape[1]
    per_gate = d // tn
    act = lambda w: pl.BlockSpec((tm, w), lambda i, j: (i, 0))
    wgt = lambda k: pl.BlockSpec((k, tn), lambda i, j: (0, j))
    gate = lambda g: pl.BlockSpec((tm, tn), lambda i, j: (i, gate_cb0 + g * per_gate + j))
    return pl.pallas_call(
        _merge_kernel,
        grid=(seq // tm, d // tn),
        in_specs=[act(o_a.shape[1]), act(o_b.shape[1]), act(o_c.shape[1]),
                  wgt(w_a.shape[0]), wgt(w_b.shape[0]), wgt(w_c.shape[0]),
                  gate(0), gate(1), gate(2)],
        out_specs=pl.BlockSpec((tm, tn), lambda i, j: (i, j)),
        out_shape=jax.ShapeDtypeStruct((seq, d), BF16),
        compiler_params=_cparams(("parallel", "arbitrary"), 48),
        name="gated_merge",
    )(o_a, o_b, o_c, w_a, w_b, w_c, proj_b, proj_b, proj_b)


def _mm_norm_res_kernel(a_ref, w_ref, x_ref, g_ref, *rest, with_next_norm):
    if with_next_norm:
        g2_ref, o_ref, h_ref = rest
    else:
        (o_ref,) = rest
    d = w_ref.shape[1]
    cols = [slice(c, c + PROJ_CHUNK) for c in range(0, d, PROJ_CHUNK)]
    accs = [jnp.dot(a_ref[...], w_ref[:, cs], preferred_element_type=F32) for cs in cols]
    ssq = sum(jnp.sum(acc * acc, axis=-1, keepdims=True) for acc in accs)
    inv = lax.rsqrt(ssq / d + EPS)
    x_new = [x_ref[:, cs] + acc * inv * g_ref[:, cs] for cs, acc in zip(cols, accs)]
    for cs, xc in zip(cols, x_new):
        o_ref[:, cs] = xc
    if with_next_norm:
        ssq2 = sum(jnp.sum(xc * xc, axis=-1, keepdims=True) for xc in x_new)
        inv2 = lax.rsqrt(ssq2 / d + EPS)
        for cs, xc in zip(cols, x_new):
            h_ref[:, cs] = (xc * inv2 * g2_ref[:, cs]).astype(h_ref.dtype)


def _mm_norm_res(a, w, x, g, g2, tm, vmem_mb, name):
    m, k = a.shape
    d = w.shape[1]
    row = lambda width: pl.BlockSpec((tm, width), lambda i: (i, 0))
    vec = pl.BlockSpec((1, d), lambda i: (0, 0))
    with_next_norm = g2 is not None
    in_specs = [row(k), pl.BlockSpec((k, d), lambda i: (0, 0), pipeline_mode=pl.Buffered(1)),
                row(d), vec]
    args = [a, w, x, g.reshape(1, d)]
    out_specs = [row(d)]
    out_shape = [jax.ShapeDtypeStruct((m, d), F32)]
    if with_next_norm:
        in_specs.append(vec)
        args.append(g2.reshape(1, d))
        out_specs.append(row(d))
        out_shape.append(jax.ShapeDtypeStruct((m, d), BF16))
    return pl.pallas_call(
        functools.partial(_mm_norm_res_kernel, with_next_norm=with_next_norm),
        grid=(m // tm,),
        in_specs=in_specs,
        out_specs=out_specs,
        out_shape=out_shape,
        compiler_params=_cparams(("parallel",), vmem_mb),
        name=name,
    )(*args)


def _swiglu_kernel(a_ref, wg_ref, wu_ref, o_ref, wg_bf_ref, wu_bf_ref):
    @pl.when(pl.program_id(1) == 0)
    def _():
        wg_bf_ref[...] = wg_ref[...].astype(BF16)
        wu_bf_ref[...] = wu_ref[...].astype(BF16)

    gate = jnp.dot(a_ref[...], wg_bf_ref[...], preferred_element_type=F32)
    up = jnp.dot(a_ref[...], wu_bf_ref[...], preferred_element_type=F32)
    o_ref[...] = (gate * jax.nn.sigmoid(gate) * up).astype(o_ref.dtype)


def _swiglu(a, w, tm, tn):
    m, k = a.shape
    d_ff = w.shape[1] // 2
    nt = d_ff // tn
    return pl.pallas_call(
        _swiglu_kernel,
        grid=(nt, m // tm),
        in_specs=[pl.BlockSpec((tm, k), lambda j, i: (i, 0)),
                  pl.BlockSpec((k, tn), lambda j, i: (0, j)),
                  pl.BlockSpec((k, tn), lambda j, i: (0, nt + j))],
        out_specs=pl.BlockSpec((tm, tn), lambda j, i: (i, j)),
        out_shape=jax.ShapeDtypeStruct((m, d_ff), BF16),
        scratch_shapes=[pltpu.VMEM((k, tn), BF16), pltpu.VMEM((k, tn), BF16)],
        compiler_params=_cparams(("arbitrary", "arbitrary"), 48),
        name="swiglu_in",
    )(a, w, w)


def _layer(x, mem, positions, g_pre_mix, g_mem, w_in, w_mem_kv, w_branch_a, w_branch_b,
           w_branch_c, w_out, g_post_mix, g_pre_ffn, w_ffn_in, w_ffn_out, g_post_ffn):
    seq, d = x.shape
    k_top = min(TOPK_MAX, seq // 4)

    sizes = (A_WIDTH, A_WIDTH, A_WIDTH, IDX_Q_WIDTH, IDX_DIM, N_IDX_HEADS,
             RET_QK_WIDTH, RET_QK_WIDTH, RET_V_WIDTH, RET_V_WIDTH, MEM_WIDTH, 3 * d)
    (o_aq, o_ak, o_av, o_iq, o_ik, o_iw, o_rq, o_rk, o_rv, o_rg, o_mq, o_gate) = (
        int(o) for o in np.cumsum((0,) + sizes[:-1]))
    w_in_t = jnp.swapaxes(w_in, 0, 1)

    tabs = _rope_tables(positions, 1024)
    (c128, s128, c64, sa64, sb64, ck, sak, sbk) = tabs
    t128 = (c128, s128, s128)
    t64 = (c64, sa64, sb64)

    h = _rmsnorm(x, g_pre_mix, 512)

    tn = 1024
    proj_iq = _proj(h, w_in_t, [o_iq, o_iq + tn // 2], t64, ROPE64, [IDX_DIM ** -0.5] * 2, BF16,
                    1024, tn, 48, "proj_index_q", pad_heads=True)
    proj_r = _proj(h, w_in_t, [o_aq, o_ak, o_rq, o_rk], t128, ROPE128,
                   [HEAD_DIM ** -0.5 * LOG2E, 1.0, 1.0, RET_QK_DIM ** -0.5], BF16, 1024, tn, 48,
                   "proj_rope")
    proj_p = _proj(h, w_in_t, [o_rv, o_rv + tn, o_av, o_mq], t128, PLAIN,
                   [1.0, 1.0, 1.0, MEM_HEAD_DIM ** -0.5], BF16, 1024, tn, 48, "proj_plain")
    gates = _proj(h, w_in_t, [o_gate + t * tn for t in range(3 * d // tn)], t128, SIGMOID,
                  [1.0] * (3 * d // tn), F32, 1024, tn, 52, "proj_gates")
    r_g = _proj(h, w_in_t, [o_rg, o_rg + tn], t128, PLAIN, [1.0] * 2, F32, 1024, tn, 52,
                "proj_ret_gate")
    assert o_iw == o_ik + IDX_DIM
    ikw = _proj(h, w_in_t, [o_ik], (ck, sak, sbk), ROPE64, [1.0], F32, 1024, LANES, 24,
                "proj_index_kw", valid_rows=IDX_DIM + N_IDX_HEADS)

    mask = _indexer_mask(ikw, proj_iq, 0, seq, k_top)
    o_a = _masked_attention(proj_r, proj_p, mask, 0, 1, 2, seq)

    o_b = _retention(proj_r, proj_p, r_g, 2, 3, 0, 0, seq)

    mem_n = _rmsnorm(mem, g_mem, mem.shape[0])
    mem_kv = _proj(mem_n, jnp.swapaxes(w_mem_kv, 0, 1), [0, tn], t128, PLAIN, [1.0] * 2, BF16,
                   mem.shape[0], tn, 40, "proj_mem_kv")
    o_c = _mem_attention(proj_p, mem_kv, 3, seq, 512)

    mixed = _merge(o_a, o_b, o_c, w_branch_a.astype(BF16), w_branch_b.astype(BF16),
                   w_branch_c.astype(BF16), gates, 0, 512, tn)
    x1, h2 = _mm_norm_res(mixed, w_out.astype(BF16), x, g_post_mix, g_pre_ffn, 512, 40,
                          "out_proj_norm_res")

    act = _swiglu(h2, w_ffn_in, 1024, 512)
    (x2,) = _mm_norm_res(act, w_ffn_out.astype(BF16), x1, g_post_ffn, None, 256, 48,
                         "ffn_out_norm_res")
    return x2


def kernel(x, mem, positions, g_pre_mix, g_mem, w_in, w_mem_kv, w_branch_a, w_branch_b, w_branch_c,
           w_out, g_post_mix, g_pre_ffn, w_ffn_in, w_ffn_out, g_post_ffn):
    assert x.shape[0] == 1 and g_pre_mix.shape[0] == 1
    out = _layer(x[0], mem[0], positions[0], g_pre_mix[0], g_mem[0], w_in[0], w_mem_kv[0],
                 w_branch_a[0], w_branch_b[0], w_branch_c[0], w_out[0], g_post_mix[0],
                 g_pre_ffn[0], w_ffn_in[0], w_ffn_out[0], g_post_ffn[0])
    return out[None]
```

```python
import functools

import jax
import jax.numpy as jnp
import numpy as np
from jax import lax
from jax.experimental import pallas as pl
from jax.experimental.pallas import tpu as pltpu

D_MODEL = 2048
N_HEADS_ATTN = 8
HEAD_DIM = 128
N_IDX_HEADS = 16
IDX_DIM = 64
TOPK_MAX = 256
N_RET_HEADS = 8
RET_QK_DIM = 128
RET_V_DIM = 256
RET_CHUNK = 128
N_MEM_HEADS = 4
MEM_HEAD_DIM = 256
ROPE_THETA = 10000.0
EPS = 1e-6

A_WIDTH = N_HEADS_ATTN * HEAD_DIM
IDX_Q_WIDTH = N_IDX_HEADS * IDX_DIM
RET_QK_WIDTH = N_RET_HEADS * RET_QK_DIM
RET_V_WIDTH = N_RET_HEADS * RET_V_DIM
MEM_WIDTH = N_MEM_HEADS * MEM_HEAD_DIM

LANES = 128
SUBLANES = 8
VMEM_BYTES_V7X = 64 * 1024 * 1024
INT_MIN = -(2 ** 31)
NEG_BIG = -1e30
LOG2E = 1.4426950408889634

F32 = jnp.float32
BF16 = jnp.bfloat16
I32 = jnp.int32

PLAIN, ROPE128, ROPE64, SIGMOID = 0, 1, 2, 3


def _cparams(sem, vmem_mb):
    assert vmem_mb * 1024 * 1024 < VMEM_BYTES_V7X
    return pltpu.CompilerParams(dimension_semantics=sem, vmem_limit_bytes=vmem_mb * 1024 * 1024)


def _rms(x, g):
    return x * lax.rsqrt(jnp.mean(x * x, axis=-1, keepdims=True) + EPS) * g


def _rmsnorm_kernel(x_ref, g_ref, o_ref):
    o_ref[...] = _rms(x_ref[...], g_ref[...]).astype(o_ref.dtype)


def _rmsnorm(x, g, tm):
    n, d = x.shape
    return pl.pallas_call(
        _rmsnorm_kernel,
        grid=(n // tm,),
        in_specs=[pl.BlockSpec((tm, d), lambda i: (i, 0)), pl.BlockSpec((1, d), lambda i: (0, 0))],
        out_specs=pl.BlockSpec((tm, d), lambda i: (i, 0)),
        out_shape=jax.ShapeDtypeStruct((n, d), BF16),
        compiler_params=_cparams(("parallel",), 24),
        name="rmsnorm",
    )(x, g.reshape(1, d))


def _rope_table_kernel(pos_ref, c_ref, c128_ref, s128_ref, c64_ref, sa64_ref, sb64_ref,
                       ck_ref, sak_ref, sbk_ref):
    pos = pos_ref[...]
    ang128 = pos * c_ref[0:1, :]
    ang64 = pos * c_ref[1:2, :]
    c128_ref[...] = jnp.cos(ang128)
    s128_ref[...] = jnp.sin(ang128) * c_ref[2:3, :]
    cos64 = jnp.cos(ang64)
    sin64 = jnp.sin(ang64)
    sa = sin64 * c_ref[3:4, :]
    sb = sin64 * c_ref[4:5, :]
    c64_ref[...] = cos64
    sa64_ref[...] = sa
    sb64_ref[...] = sb
    ck_ref[...] = cos64 * c_ref[5:6, :] + c_ref[6:7, :]
    sak_ref[...] = sa * c_ref[5:6, :]
    sbk_ref[...] = sb * c_ref[5:6, :]


def _rope_tables(positions, tm):
    s = positions.shape[-1]
    pos_b = jnp.broadcast_to(positions.reshape(s, 1).astype(F32), (s, LANES))
    lane = np.arange(LANES)
    f128 = ROPE_THETA ** (-jnp.arange(0, HEAD_DIM, 2, dtype=F32) / HEAD_DIM)
    f64 = ROPE_THETA ** (-jnp.arange(0, IDX_DIM, 2, dtype=F32) / IDX_DIM)
    half64 = (lane % IDX_DIM) < IDX_DIM // 2
    rows = [
        jnp.tile(f128, 2),
        jnp.tile(f64, 4),
        jnp.asarray(np.where(lane < HEAD_DIM // 2, -1.0, 1.0), F32),
        jnp.asarray(np.where(half64, -1.0, 0.0), F32),
        jnp.asarray(np.where(half64, 0.0, 1.0), F32),
        jnp.asarray(np.where(lane < IDX_DIM, 1.0, 0.0), F32),
        jnp.asarray(np.where(lane < IDX_DIM, 0.0,
                             np.where(lane < IDX_DIM + N_IDX_HEADS, N_IDX_HEADS ** -0.5, 1.0)), F32),
        jnp.zeros((LANES,), F32),
    ]
    consts = jnp.stack(rows)
    spec = pl.BlockSpec((tm, LANES), lambda i: (i, 0))
    return pl.pallas_call(
        _rope_table_kernel,
        grid=(s // tm,),
        in_specs=[spec, pl.BlockSpec((SUBLANES, LANES), lambda i: (0, 0))],
        out_specs=[spec] * 8,
        out_shape=[jax.ShapeDtypeStruct((s, LANES), F32)] * 8,
        compiler_params=_cparams(("parallel",), 24),
        name="rope_tables",
    )(pos_b, consts)


def _rope128(x, cos, sin_signed):
    return x * cos + pltpu.roll(x, HEAD_DIM // 2, 1) * sin_signed


def _rope64(x, cos, sin_a, sin_b):
    return (x * cos + pltpu.roll(x, LANES - IDX_DIM // 2, 1) * sin_a
            + pltpu.roll(x, IDX_DIM // 2, 1) * sin_b)


PROJ_CHUNK = 2 * LANES


def _select_by_tile(j, values, dtype):
    out = jnp.asarray(values[-1], dtype)
    for t in range(len(values) - 2, -1, -1):
        out = jnp.where(j == t, jnp.asarray(values[t], dtype), out)
    return out


def _proj_kernel(a_ref, wt_ref, t0_ref, t1_ref, t2_ref, o_ref, wbf_ref, *, mode, scales, tn,
                 valid_rows, pad_heads):
    j = pl.program_id(0)

    @pl.when(pl.program_id(1) == 0)
    def _():
        if pad_heads:
            zeros = jnp.zeros((LANES - IDX_DIM, wbf_ref.shape[1]), BF16)
            for h in range(tn // LANES):
                wbf_ref[h * LANES:h * LANES + IDX_DIM, :] = (
                    wt_ref[h * IDX_DIM:(h + 1) * IDX_DIM, :].astype(BF16))
                wbf_ref[h * LANES + IDX_DIM:(h + 1) * LANES, :] = zeros
        else:
            w = wt_ref[...]
            if valid_rows < tn:
                w = jnp.where(lax.broadcasted_iota(I32, w.shape, 0) < valid_rows, w, 0.0)
            wbf_ref[...] = w.astype(BF16)

    scale = _select_by_tile(j, scales, F32)
    uniform = all(s == scales[0] for s in scales)
    cw = min(PROJ_CHUNK, tn)
    for c in range(tn // cw):
        acc = lax.dot_general(a_ref[...], wbf_ref[c * cw:(c + 1) * cw, :], (((1,), (1,)), ((), ())),
                              preferred_element_type=F32)
        for s in range(cw // LANES):
            x = acc[:, s * LANES:(s + 1) * LANES]
            if mode == ROPE128:
                x = _rope128(x, t0_ref[...], t1_ref[...])
            elif mode == ROPE64:
                x = _rope64(x, t0_ref[...], t1_ref[...], t2_ref[...])
            elif mode == SIGMOID:
                x = jax.nn.sigmoid(x)
            if not (uniform and scales[0] == 1.0):
                x = x * scale
            col = c * cw + s * LANES
            o_ref[:, col:col + LANES] = x.astype(o_ref.dtype)


def _proj(a, wt, row_offsets, tables, mode, scales, out_dtype, tm, tn, vmem_mb, name, *,
          valid_rows=None, pad_heads=False):
    m, k = a.shape
    assert wt.shape[1] == k and len(row_offsets) == len(scales) and tn % min(PROJ_CHUNK, tn) == 0
    src_rows = tn // 2 if pad_heads else tn
    valid_rows = tn if valid_rows is None else valid_rows
    assert all(o % SUBLANES == 0 and o + src_rows <= wt.shape[0] for o in row_offsets)
    tspec = pl.BlockSpec((tm, LANES), lambda j, i: (i, 0))
    return pl.pallas_call(
        functools.partial(_proj_kernel, mode=mode, scales=tuple(scales), tn=tn,
                          valid_rows=valid_rows, pad_heads=pad_heads),
        grid=(len(row_offsets), m // tm),
        in_specs=[pl.BlockSpec((tm, k), lambda j, i: (i, 0)),
                  pl.BlockSpec((pl.Element(src_rows), pl.Element(k)),
                               lambda j, i: (_select_by_tile(
                                   j, [o // SUBLANES for o in row_offsets], I32) * SUBLANES, 0)),
                  tspec, tspec, tspec],
        out_specs=pl.BlockSpec((tm, tn), lambda j, i: (i, j)),
        out_shape=jax.ShapeDtypeStruct((m, len(row_offsets) * tn), out_dtype),
        scratch_shapes=[pltpu.VMEM((tn, k), BF16)],
        compiler_params=_cparams(("arbitrary", "arbitrary"), vmem_mb),
        name=name,
    )(a, wt, *tables)


IDX_TQ = 256
IDX_KT = 512
IDX_PARTS = 4
IDX_BITS = 13
IDX_UNCHECKED_PASSES = 18
IDX_BRACKET_SLACK = 1.0 / 16


def _f32_key(x):
    bits = lax.bitcast_convert_type(x, I32)
    return bits ^ (lax.shift_right_arithmetic(bits, 31) & jnp.int32(0x7FFFFFFF))


def _key_f32(key):
    bits = key ^ (lax.shift_right_arithmetic(key, 31) & jnp.int32(0x7FFFFFFF))
    return lax.bitcast_convert_type(bits, F32)


def _indexer_kernel(ikw_ref, iq_ref, mask_ref, s_ref, gmax_ref, *, seq, k_top):
    b = pl.program_id(0)
    tq, kt = IDX_TQ, IDX_KT
    pr = kt // IDX_PARTS
    q0 = b * tq
    ntile = q0 // kt + 1

    w_t = jnp.transpose(ikw_ref[pl.ds(pl.multiple_of(q0, tq), tq), :])
    qidx = q0 + lax.broadcasted_iota(I32, (kt, tq), 1)
    gmax_ref[...] = jnp.full(gmax_ref.shape, -jnp.inf, F32)

    def score_tile(t, carry):
        r0 = pl.multiple_of(t * kt, kt)
        ik = ikw_ref[pl.ds(r0, kt), :].astype(BF16)
        acc = jnp.zeros((kt, tq), F32)
        for h in range(N_IDX_HEADS):
            qh = iq_ref[:, h * LANES:(h + 1) * LANES]
            s = lax.dot_general(ik, qh, (((1,), (1,)), ((), ())), preferred_element_type=F32)
            acc = acc + jnp.maximum(s, 0.0) * w_t[IDX_DIM + h:IDX_DIM + h + 1, :]
        kidx = r0 + lax.broadcasted_iota(I32, (kt, tq), 0)
        sc = jnp.where(kidx <= qidx, acc, -jnp.inf)
        s_ref[pl.ds(r0, kt), :] = sc
        gmax_ref[...] = functools.reduce(
            jnp.maximum, [sc[g * k_top:(g + 1) * k_top] for g in range(kt // k_top)], gmax_ref[...])
        return carry

    lax.fori_loop(0, ntile, score_tile, 0)

    def count(pred):
        def body(t, cnt):
            parts = []
            for part in range(IDX_PARTS):
                r0 = pl.multiple_of(t * kt + part * pr, pr)
                idx = r0 + lax.broadcasted_iota(I32, (pr, tq), 0)
                hit = pred(s_ref[pl.ds(r0, pr), :], idx)
                parts.append(jnp.sum(hit.reshape(pr // SUBLANES, SUBLANES, tq), axis=0))
            return cnt + ((parts[0] + parts[1]) + (parts[2] + parts[3]))
        cnt = lax.fori_loop(0, ntile, body, jnp.zeros((SUBLANES, tq), I32))
        return jnp.sum(cnt, axis=0, keepdims=True)

    g = gmax_ref[...]
    g_lo = jnp.min(g, axis=0, keepdims=True)
    g_hi = jnp.max(g, axis=0, keepdims=True)
    slack = jnp.maximum(jnp.abs(g_lo), jnp.abs(g_hi)) * IDX_BRACKET_SLACK
    lo0 = _f32_key(g_lo - slack)
    hi0 = _f32_key(g_hi + slack)

    def search_pass(lo, hi, split):
        mid = (lo | hi) - lax.shift_right_arithmetic(lo ^ hi, 1)
        cand = _key_f32(mid)
        cnt = count(lambda sc, idx: jnp.where(sc >= cand, 1, 0))
        exact = cnt == k_top
        enough = cnt >= k_top
        lo = jnp.where(enough, mid, lo)
        hi = jnp.where(exact, mid, jnp.where(enough, hi, mid - 1))
        return lo, hi, jnp.maximum(split, jnp.where(exact, 1, 0))

    state = lax.fori_loop(0, IDX_UNCHECKED_PASSES, lambda p, c: search_pass(*c),
                          (lo0, hi0, jnp.zeros((1, tq), I32)))

    def checked_passes(carry):
        lo, hi, split = search_pass(*search_pass(*carry[:3]))
        return lo, hi, split, jnp.max(jnp.where(lo == hi, 0, 1))

    lo, hi, split, _ = lax.while_loop(
        lambda c: c[3] > 0, checked_passes,
        (*state, jnp.max(jnp.where(state[0] == state[1], 0, 1))))
    thr = _key_f32(lo)

    def write_mask_tiles(select):
        def body(t, carry):
            r0 = pl.multiple_of(t * kt, kt)
            idx = r0 + lax.broadcasted_iota(I32, (kt, tq), 0)
            sel = select(s_ref[pl.ds(r0, kt), :], idx)
            mask_ref[t] = jnp.transpose(sel).astype(mask_ref.dtype)
            return carry
        lax.fori_loop(0, ntile, body, 0)

    def separated():
        write_mask_tiles(lambda sc, idx: jnp.where(sc >= thr, 1.0, 0.0))

    def with_ties():
        cnt_gt = count(lambda sc, idx: jnp.where(sc > thr, 1, 0))
        cnt_eq = count(lambda sc, idx: jnp.where(sc == thr, 1, 0))
        need = k_top - cnt_gt
        finite = jnp.where(thr > -jnp.inf, 1, 0)
        tied = finite * jnp.where(cnt_eq > need, 1, 0)

        def tie_break():
            def index_pass(p, x):
                cand = x | lax.shift_left(jnp.int32(1), IDX_BITS - 1 - p)
                cnt = count(lambda sc, idx: jnp.where(sc == thr, jnp.where(idx < cand, 1, 0), 0))
                return jnp.where(cnt < need, cand, x)
            return lax.fori_loop(0, IDX_BITS, index_pass, jnp.zeros((1, tq), I32))

        last = lax.cond(jnp.max(tied) > 0, tie_break, lambda: jnp.full((1, tq), seq, I32))
        last = jnp.where(tied > 0, last, seq)
        eq_val = finite.astype(F32)
        write_mask_tiles(lambda sc, idx: jnp.where(
            sc > thr, 1.0, jnp.where(sc == thr, jnp.where(idx <= last, eq_val, 0.0), 0.0)))

    lax.cond(jnp.min(split) > 0, separated, with_ties)

    def clear_rest(t, carry):
        mask_ref[t] = jnp.zeros((tq, kt), mask_ref.dtype)
        return carry

    lax.fori_loop(ntile, seq // kt, clear_rest, 0)


def _indexer_mask(ikw, proj_a, iq_col_block, seq, k_top):
    return pl.pallas_call(
        functools.partial(_indexer_kernel, seq=seq, k_top=k_top),
        grid=(seq // IDX_TQ,),
        in_specs=[pl.BlockSpec((seq, LANES), lambda b: (0, 0)),
                  pl.BlockSpec((IDX_TQ, N_IDX_HEADS * LANES), lambda b: (b, iq_col_block))],
        out_specs=pl.BlockSpec((seq // IDX_KT, IDX_TQ, IDX_KT), lambda b: (0, b, 0)),
        out_shape=jax.ShapeDtypeStruct((seq // IDX_KT, seq, IDX_KT), BF16),
        scratch_shapes=[pltpu.VMEM((seq, IDX_TQ), F32), pltpu.VMEM((k_top, IDX_TQ), F32)],
        compiler_params=_cparams(("parallel",), 48),
        name="indexer_topk_mask",
    )(ikw, proj_a)


ATT_TQ = 512
ATT_KT = 512


def _attn_kernel(q_ref, k_ref, v_ref, mask_ref, o_ref, m_ref, acc_ref):
    i = pl.program_id(0)
    j = pl.program_id(1)

    @pl.when(j == 0)
    def _():
        m_ref[...] = jnp.full(m_ref.shape, NEG_BIG, F32)
        acc_ref[...] = jnp.zeros(acc_ref.shape, F32)

    @pl.when(j <= i)
    def _():
        bias = (1.0 - mask_ref[0].astype(F32)) * NEG_BIG
        for h in range(N_HEADS_ATTN):
            cs = slice(h * HEAD_DIM, (h + 1) * HEAD_DIM)
            s = lax.dot_general(q_ref[:, cs], k_ref[:, cs], (((1,), (1,)), ((), ())),
                                preferred_element_type=F32) + bias
            m_prev = m_ref[h]
            m_new = jnp.maximum(m_prev, jnp.max(s, axis=1, keepdims=True))
            alpha = jnp.exp2(m_prev - m_new)
            p = jnp.exp2(s - jnp.concatenate([m_new] * (s.shape[1] // LANES), axis=1))
            m_ref[h] = m_new
            v_aug = jnp.concatenate([v_ref[:, cs], jnp.ones((v_ref.shape[0], LANES), BF16)], axis=1)
            ca = slice(2 * h * HEAD_DIM, 2 * (h + 1) * HEAD_DIM)
            acc_ref[:, ca] = (jnp.concatenate([alpha, alpha], axis=1) * acc_ref[:, ca]
                              + jnp.dot(p.astype(BF16), v_aug, preferred_element_type=F32))

    @pl.when(j == i)
    def _():
        for h in range(N_HEADS_ATTN):
            cs = slice(h * HEAD_DIM, (h + 1) * HEAD_DIM)
            lo = 2 * h * HEAD_DIM
            o_ref[:, cs] = (acc_ref[:, lo:lo + HEAD_DIM]
                            / acc_ref[:, lo + HEAD_DIM:lo + 2 * HEAD_DIM]).astype(o_ref.dtype)


def _masked_attention(proj_qk, proj_v, mask, q_cb, k_cb, v_cb, seq):
    assert ATT_KT == IDX_KT and mask.shape == (seq // ATT_KT, seq, ATT_KT)
    tq, kt = ATT_TQ, ATT_KT
    return pl.pallas_call(
        _attn_kernel,
        grid=(seq // tq, seq // kt),
        in_specs=[pl.BlockSpec((tq, A_WIDTH), lambda i, j: (i, q_cb)),
                  pl.BlockSpec((kt, A_WIDTH), lambda i, j: (jnp.minimum(i, j), k_cb)),
                  pl.BlockSpec((kt, A_WIDTH), lambda i, j: (jnp.minimum(i, j), v_cb)),
                  pl.BlockSpec((1, tq, kt), lambda i, j: (jnp.minimum(i, j), i, 0))],
        out_specs=pl.BlockSpec((tq, A_WIDTH), lambda i, j: (i, 0)),
        out_shape=jax.ShapeDtypeStruct((seq, A_WIDTH), BF16),
        scratch_shapes=[pltpu.VMEM((N_HEADS_ATTN, tq, LANES), F32),
                        pltpu.VMEM((tq, 2 * A_WIDTH), F32)],
        compiler_params=_cparams(("parallel", "arbitrary"), 40),
        name="masked_attention",
    )(proj_qk, proj_qk, proj_v, mask)


def _retention_kernel(q_ref, k_ref, v_ref, g_ref, idec_ref, qdec_ref, kdec_ref, cdec_ref,
                      o_ref, state_ref):
    @pl.when(pl.program_id(0) == 0)
    def _():
        state_ref[...] = jnp.zeros(state_ref.shape, F32)

    for h in range(N_RET_HEADS):
        ck = slice(h * RET_QK_DIM, (h + 1) * RET_QK_DIM)
        cv = slice(h * RET_V_DIM, (h + 1) * RET_V_DIM)
        q = q_ref[:, ck]
        k = k_ref[:, ck]
        v = v_ref[:, cv]
        scores = lax.dot_general(q, k, (((1,), (1,)), ((), ())),
                                 preferred_element_type=F32) * idec_ref[h]
        inner = jnp.dot(scores.astype(BF16), v, preferred_element_type=F32)
        state = state_ref[h]
        cross = jnp.dot(q, state.astype(BF16), preferred_element_type=F32) * qdec_ref[h]
        y = inner + cross
        mu = jnp.mean(y, axis=-1, keepdims=True)
        var = jnp.mean(jnp.square(y - mu), axis=-1, keepdims=True)
        yn = (y - mu) * lax.rsqrt(var + EPS)
        g = g_ref[:, cv]
        o_ref[:, cv] = (g * jax.nn.sigmoid(g) * yn).astype(o_ref.dtype)
        k_dec_t = jnp.transpose(k.astype(F32) * kdec_ref[h]).astype(BF16)
        state_ref[h] = state * cdec_ref[h] + jnp.dot(k_dec_t, v, preferred_element_type=F32)


def _retention(proj_qk, proj_v, proj_g, q_cb, k_cb, v_cb, g_cb, seq):
    c = RET_CHUNK
    h = N_RET_HEADS
    log_gamma = jnp.log1p(-jnp.exp2(-5.0 - jnp.arange(h, dtype=F32)))
    i = jnp.arange(c, dtype=F32)
    diff = i[:, None] - i[None, :]
    idec = jnp.where(diff[None] >= 0,
                     jnp.exp(jnp.maximum(diff, 0.0)[None] * log_gamma[:, None, None]), 0.0)
    kdec = jnp.exp((c - 1 - i)[None, :] * log_gamma[:, None])
    qdec = jnp.exp((i + 1)[None, :] * log_gamma[:, None])
    cdec = jnp.exp(c * log_gamma)
    qdec_b = jnp.broadcast_to(qdec[:, :, None], (h, c, RET_V_DIM))
    kdec_b = jnp.broadcast_to(kdec[:, :, None], (h, c, RET_QK_DIM))
    cdec_b = jnp.broadcast_to(cdec[:, None, None], (h, RET_QK_DIM, RET_V_DIM))
    const = lambda shape: pl.BlockSpec(shape, lambda n: (0, 0, 0))
    return pl.pallas_call(
        _retention_kernel,
        grid=(seq // c,),
        in_specs=[pl.BlockSpec((c, RET_QK_WIDTH), lambda n: (n, q_cb)),
                  pl.BlockSpec((c, RET_QK_WIDTH), lambda n: (n, k_cb)),
                  pl.BlockSpec((c, RET_V_WIDTH), lambda n: (n, v_cb)),
                  pl.BlockSpec((c, RET_V_WIDTH), lambda n: (n, g_cb)),
                  const((h, c, c)), const((h, c, RET_V_DIM)), const((h, c, RET_QK_DIM)),
                  const((h, RET_QK_DIM, RET_V_DIM))],
        out_specs=pl.BlockSpec((c, RET_V_WIDTH), lambda n: (n, 0)),
        out_shape=jax.ShapeDtypeStruct((seq, RET_V_WIDTH), BF16),
        scratch_shapes=[pltpu.VMEM((h, RET_QK_DIM, RET_V_DIM), F32)],
        compiler_params=_cparams(("arbitrary",), 24),
        name="retention",
    )(proj_qk, proj_qk, proj_v, proj_g, idec, qdec_b, kdec_b, cdec_b)


def _mem_attn_kernel(q_ref, kv_ref, o_ref):
    for h in range(N_MEM_HEADS):
        cs = slice(h * MEM_HEAD_DIM, (h + 1) * MEM_HEAD_DIM)
        vs = slice(MEM_WIDTH + h * MEM_HEAD_DIM, MEM_WIDTH + (h + 1) * MEM_HEAD_DIM)
        s = lax.dot_general(q_ref[:, cs], kv_ref[:, cs], (((1,), (1,)), ((), ())),
                            preferred_element_type=F32)
        e = jnp.exp(s - jnp.max(s, axis=-1, keepdims=True))
        p = e / jnp.sum(e, axis=-1, keepdims=True)
        o_ref[:, cs] = jnp.dot(p.astype(BF16), kv_ref[:, vs],
                               preferred_element_type=F32).astype(o_ref.dtype)


def _mem_attention(proj_a, mem_kv, q_cb, seq, tm):
    n_mem = mem_kv.shape[0]
    return pl.pallas_call(
        _mem_attn_kernel,
        grid=(seq // tm,),
        in_specs=[pl.BlockSpec((tm, MEM_WIDTH), lambda i: (i, q_cb)),
                  pl.BlockSpec((n_mem, 2 * MEM_WIDTH), lambda i: (0, 0))],
        out_specs=pl.BlockSpec((tm, MEM_WIDTH), lambda i: (i, 0)),
        out_shape=jax.ShapeDtypeStruct((seq, MEM_WIDTH), BF16),
        compiler_params=_cparams(("parallel",), 24),
        name="memory_attention",
    )(proj_a, mem_kv)


def _merge_kernel(oa_ref, ob_ref, oc_ref, wa_ref, wb_ref, wc_ref, g0_ref, g1_ref, g2_ref, o_ref):
    for c in range(0, o_ref.shape[1], PROJ_CHUNK):
        cs = slice(c, c + PROJ_CHUNK)
        mixed = g0_ref[:, cs] * jnp.dot(oa_ref[...], wa_ref[:, cs], preferred_element_type=F32)
        mixed = mixed + g1_ref[:, cs] * jnp.dot(ob_ref[...], wb_ref[:, cs], preferred_element_type=F32)
        mixed = mixed + g2_ref[:, cs] * jnp.dot(oc_ref[...], wc_ref[:, cs], preferred_element_type=F32)
        o_ref[:, cs] = mixed.astype(o_ref.dtype)


def _merge(o_a, o_b, o_c, w_a, w_b, w_c, proj_b, gate_cb0, tm, tn):
    seq = o_a.shape[0]
    d = w_a.shape[1]
    per_gate = d // tn
    act = lambda w: pl.BlockSpec((tm, w), lambda i, j: (i, 0))
    wgt = lambda k: pl.BlockSpec((k, tn), lambda i, j: (0, j))
    gate = lambda g: pl.BlockSpec((tm, tn), lambda i, j: (i, gate_cb0 + g * per_gate + j))
    return pl.pallas_call(
        _merge_kernel,
        grid=(seq // tm, d // tn),
        in_specs=[act(o_a.shape[1]), act(o_b.shape[1]), act(o_c.shape[1]),
                  wgt(w_a.shape[0]), wgt(w_b.shape[0]), wgt(w_c.shape[0]),
                  gate(0), gate(1), gate(2)],
        out_specs=pl.BlockSpec((tm, tn), lambda i, j: (i, j)),
        out_shape=jax.ShapeDtypeStruct((seq, d), BF16),
        compiler_params=_cparams(("parallel", "arbitrary"), 48),
        name="gated_merge",
    )(o_a, o_b, o_c, w_a, w_b, w_c, proj_b, proj_b, proj_b)


def _mm_norm_res_kernel(a_ref, w_ref, x_ref, g_ref, *rest, with_next_norm):
    if with_next_norm:
        g2_ref, o_ref, h_ref = rest
    else:
        (o_ref,) = rest
    d = w_ref.shape[1]
    cols = [slice(c, c + PROJ_CHUNK) for c in range(0, d, PROJ_CHUNK)]
    accs = [jnp.dot(a_ref[...], w_ref[:, cs], preferred_element_type=F32) for cs in cols]
    ssq = sum(jnp.sum(acc * acc, axis=-1, keepdims=True) for acc in accs)
    inv = lax.rsqrt(ssq / d + EPS)
    x_new = [x_ref[:, cs] + acc * inv * g_ref[:, cs] for cs, acc in zip(cols, accs)]
    for cs, xc in zip(cols, x_new):
        o_ref[:, cs] = xc
    if with_next_norm:
        ssq2 = sum(jnp.sum(xc * xc, axis=-1, keepdims=True) for xc in x_new)
        inv2 = lax.rsqrt(ssq2 / d + EPS)
        for cs, xc in zip(cols, x_new):
            h_ref[:, cs] = (xc * inv2 * g2_ref[:, cs]).astype(h_ref.dtype)


def _mm_norm_res(a, w, x, g, g2, tm, vmem_mb, name):
    m, k = a.shape
    d = w.shape[1]
    row = lambda width: pl.BlockSpec((tm, width), lambda i: (i, 0))
    vec = pl.BlockSpec((1, d), lambda i: (0, 0))
    with_next_norm = g2 is not None
    in_specs = [row(k), pl.BlockSpec((k, d), lambda i: (0, 0), pipeline_mode=pl.Buffered(1)),
                row(d), vec]
    args = [a, w, x, g.reshape(1, d)]
    out_specs = [row(d)]
    out_shape = [jax.ShapeDtypeStruct((m, d), F32)]
    if with_next_norm:
        in_specs.append(vec)
        args.append(g2.reshape(1, d))
        out_specs.append(row(d))
        out_shape.append(jax.ShapeDtypeStruct((m, d), BF16))
    return pl.pallas_call(
        functools.partial(_mm_norm_res_kernel, with_next_norm=with_next_norm),
        grid=(m // tm,),
        in_specs=in_specs,
        out_specs=out_specs,
        out_shape=out_shape,
        compiler_params=_cparams(("parallel",), vmem_mb),
        name=name,
    )(*args)


def _swiglu_kernel(a_ref, wg_ref, wu_ref, o_ref, wg_bf_ref, wu_bf_ref):
    @pl.when(pl.program_id(1) == 0)
    def _():
        wg_bf_ref[...] = wg_ref[...].astype(BF16)
        wu_bf_ref[...] = wu_ref[...].astype(BF16)

    gate = jnp.dot(a_ref[...], wg_bf_ref[...], preferred_element_type=F32)
    up = jnp.dot(a_ref[...], wu_bf_ref[...], preferred_element_type=F32)
    o_ref[...] = (gate * jax.nn.sigmoid(gate) * up).astype(o_ref.dtype)


def _swiglu(a, w, tm, tn):
    m, k = a.shape
    d_ff = w.shape[1] // 2
    nt = d_ff // tn
    return pl.pallas_call(
        _swiglu_kernel,
        grid=(nt, m // tm),
        in_specs=[pl.BlockSpec((tm, k), lambda j, i: (i, 0)),
                  pl.BlockSpec((k, tn), lambda j, i: (0, j)),
                  pl.BlockSpec((k, tn), lambda j, i: (0, nt + j))],
        out_specs=pl.BlockSpec((tm, tn), lambda j, i: (i, j)),
        out_shape=jax.ShapeDtypeStruct((m, d_ff), BF16),
        scratch_shapes=[pltpu.VMEM((k, tn), BF16), pltpu.VMEM((k, tn), BF16)],
        compiler_params=_cparams(("arbitrary", "arbitrary"), 48),
        name="swiglu_in",
    )(a, w, w)


def _layer(x, mem, positions, g_pre_mix, g_mem, w_in, w_mem_kv, w_branch_a, w_branch_b,
           w_branch_c, w_out, g_post_mix, g_pre_ffn, w_ffn_in, w_ffn_out, g_post_ffn):
    seq, d = x.shape
    k_top = min(TOPK_MAX, seq // 4)

    sizes = (A_WIDTH, A_WIDTH, A_WIDTH, IDX_Q_WIDTH, IDX_DIM, N_IDX_HEADS,
             RET_QK_WIDTH, RET_QK_WIDTH, RET_V_WIDTH, RET_V_WIDTH, MEM_WIDTH, 3 * d)
    (o_aq, o_ak, o_av, o_iq, o_ik, o_iw, o_rq, o_rk, o_rv, o_rg, o_mq, o_gate) = (
        int(o) for o in np.cumsum((0,) + sizes[:-1]))
    w_in_t = jnp.swapaxes(w_in, 0, 1)

    tabs = _rope_tables(positions, 1024)
    (c128, s128, c64, sa64, sb64, ck, sak, sbk) = tabs
    t128 = (c128, s128, s128)
    t64 = (c64, sa64, sb64)

    h = _rmsnorm(x, g_pre_mix, 512)

    tn = 1024
    proj_iq = _proj(h, w_in_t, [o_iq, o_iq + tn // 2], t64, ROPE64, [IDX_DIM ** -0.5] * 2, BF16,
                    1024, tn, 48, "proj_index_q", pad_heads=True)
    proj_r = _proj(h, w_in_t, [o_aq, o_ak, o_rq, o_rk], t128, ROPE128,
                   [HEAD_DIM ** -0.5 * LOG2E, 1.0, 1.0, RET_QK_DIM ** -0.5], BF16, 1024, tn, 48,
                   "proj_rope")
    proj_p = _proj(h, w_in_t, [o_rv, o_rv + tn, o_av, o_mq], t128, PLAIN,
                   [1.0, 1.0, 1.0, MEM_HEAD_DIM ** -0.5], BF16, 1024, tn, 48, "proj_plain")
    gates = _proj(h, w_in_t, [o_gate + t * tn for t in range(3 * d // tn)], t128, SIGMOID,
                  [1.0] * (3 * d // tn), F32, 1024, tn, 52, "proj_gates")
    r_g = _proj(h, w_in_t, [o_rg, o_rg + tn], t128, PLAIN, [1.0] * 2, F32, 1024, tn, 52,
                "proj_ret_gate")
    assert o_iw == o_ik + IDX_DIM
    ikw = _proj(h, w_in_t, [o_ik], (ck, sak, sbk), ROPE64, [1.0], F32, 1024, LANES, 24,
                "proj_index_kw", valid_rows=IDX_DIM + N_IDX_HEADS)

    mask = _indexer_mask(ikw, proj_iq, 0, seq, k_top)
    o_a = _masked_attention(proj_r, proj_p, mask, 0, 1, 2, seq)

    o_b = _retention(proj_r, proj_p, r_g, 2, 3, 0, 0, seq)

    mem_n = _rmsnorm(mem, g_mem, mem.shape[0])
    mem_kv = _proj(mem_n, jnp.swapaxes(w_mem_kv, 0, 1), [0, tn], t128, PLAIN, [1.0] * 2, BF16,
                   mem.shape[0], tn, 40, "proj_mem_kv")
    o_c = _mem_attention(proj_p, mem_kv, 3, seq, 512)

    mixed = _merge(o_a, o_b, o_c, w_branch_a.astype(BF16), w_branch_b.astype(BF16),
                   w_branch_c.astype(BF16), gates, 0, 512, tn)
    x1, h2 = _mm_norm_res(mixed, w_out.astype(BF16), x, g_post_mix, g_pre_ffn, 512, 40,
                          "out_proj_norm_res")

    act = _swiglu(h2, w_ffn_in, 1024, 512)
    (x2,) = _mm_norm_res(act, w_ffn_out.astype(BF16), x1, g_post_ffn, None, 256, 48,
                         "ffn_out_norm_res")
    return x2


def kernel(x, mem, positions, g_pre_mix, g_mem, w_in, w_mem_kv, w_branch_a, w_branch_b, w_branch_c,
           w_out, g_post_mix, g_pre_ffn, w_ffn_in, w_ffn_out, g_post_ffn):
    assert x.shape[0] == 1 and g_pre_mix.shape[0] == 1
    out = _layer(x[0], mem[0], positions[0], g_pre_mix[0], g_mem[0], w_in[0], w_mem_kv[0],
                 w_branch_a[0], w_branch_b[0], w_branch_c[0], w_out[0], g_post_mix[0],
                 g_pre_ffn[0], w_ffn_in[0], w_ffn_out[0], g_post_ffn[0])
    return out[None]
```

```python
import functools

import jax
import jax.numpy as jnp
import numpy as np
from jax import lax
from jax.experimental import pallas as pl
from jax.experimental.pallas import tpu as pltpu

D_MODEL = 2048
N_HEADS_ATTN = 8
HEAD_DIM = 128
N_IDX_HEADS = 16
IDX_DIM = 64
TOPK_MAX = 256
N_RET_HEADS = 8
RET_QK_DIM = 128
RET_V_DIM = 256
RET_CHUNK = 128
N_MEM_HEADS = 4
MEM_HEAD_DIM = 256
ROPE_THETA = 10000.0
EPS = 1e-6

A_WIDTH = N_HEADS_ATTN * HEAD_DIM
IDX_Q_WIDTH = N_IDX_HEADS * IDX_DIM
RET_QK_WIDTH = N_RET_HEADS * RET_QK_DIM
RET_V_WIDTH = N_RET_HEADS * RET_V_DIM
MEM_WIDTH = N_MEM_HEADS * MEM_HEAD_DIM

LANES = 128
SUBLANES = 8
VMEM_BYTES_V7X = 64 * 1024 * 1024
INT_MIN = -(2 ** 31)
NEG_BIG = -1e30
LOG2E = 1.4426950408889634

F32 = jnp.float32
BF16 = jnp.bfloat16
I32 = jnp.int32

PLAIN, ROPE128, ROPE64, SIGMOID = 0, 1, 2, 3


def _cparams(sem, vmem_mb):
    assert vmem_mb * 1024 * 1024 < VMEM_BYTES_V7X
    return pltpu.CompilerParams(dimension_semantics=sem, vmem_limit_bytes=vmem_mb * 1024 * 1024)


def _rms(x, g):
    return x * lax.rsqrt(jnp.mean(x * x, axis=-1, keepdims=True) + EPS) * g


def _rmsnorm_kernel(x_ref, g_ref, o_ref):
    o_ref[...] = _rms(x_ref[...], g_ref[...]).astype(o_ref.dtype)


def _rmsnorm(x, g, tm):
    n, d = x.shape
    return pl.pallas_call(
        _rmsnorm_kernel,
        grid=(n // tm,),
        in_specs=[pl.BlockSpec((tm, d), lambda i: (i, 0)), pl.BlockSpec((1, d), lambda i: (0, 0))],
        out_specs=pl.BlockSpec((tm, d), lambda i: (i, 0)),
        out_shape=jax.ShapeDtypeStruct((n, d), BF16),
        compiler_params=_cparams(("parallel",), 24),
        name="rmsnorm",
    )(x, g.reshape(1, d))


def _rope_table_kernel(pos_ref, c_ref, c128_ref, s128_ref, c64_ref, sa64_ref, sb64_ref,
                       ck_ref, sak_ref, sbk_ref):
    pos = pos_ref[...]
    ang128 = pos * c_ref[0:1, :]
    ang64 = pos * c_ref[1:2, :]
    c128_ref[...] = jnp.cos(ang128)
    s128_ref[...] = jnp.sin(ang128) * c_ref[2:3, :]
    cos64 = jnp.cos(ang64)
    sin64 = jnp.sin(ang64)
    sa = sin64 * c_ref[3:4, :]
    sb = sin64 * c_ref[4:5, :]
    c64_ref[...] = cos64
    sa64_ref[...] = sa
    sb64_ref[...] = sb
    ck_ref[...] = cos64 * c_ref[5:6, :] + c_ref[6:7, :]
    sak_ref[...] = sa * c_ref[5:6, :]
    sbk_ref[...] = sb * c_ref[5:6, :]


def _rope_tables(positions, tm):
    s = positions.shape[-1]
    pos_b = jnp.broadcast_to(positions.reshape(s, 1).astype(F32), (s, LANES))
    lane = np.arange(LANES)
    f128 = ROPE_THETA ** (-jnp.arange(0, HEAD_DIM, 2, dtype=F32) / HEAD_DIM)
    f64 = ROPE_THETA ** (-jnp.arange(0, IDX_DIM, 2, dtype=F32) / IDX_DIM)
    half64 = (lane % IDX_DIM) < IDX_DIM // 2
    rows = [
        jnp.tile(f128, 2),
        jnp.tile(f64, 4),
        jnp.asarray(np.where(lane < HEAD_DIM // 2, -1.0, 1.0), F32),
        jnp.asarray(np.where(half64, -1.0, 0.0), F32),
        jnp.asarray(np.where(half64, 0.0, 1.0), F32),
        jnp.asarray(np.where(lane < IDX_DIM, 1.0, 0.0), F32),
        jnp.asarray(np.where(lane < IDX_DIM, 0.0,
                             np.where(lane < IDX_DIM + N_IDX_HEADS, N_IDX_HEADS ** -0.5, 1.0)), F32),
        jnp.zeros((LANES,), F32),
    ]
    consts = jnp.stack(rows)
    spec = pl.BlockSpec((tm, LANES), lambda i: (i, 0))
    return pl.pallas_call(
        _rope_table_kernel,
        grid=(s // tm,),
        in_specs=[spec, pl.BlockSpec((SUBLANES, LANES), lambda i: (0, 0))],
        out_specs=[spec] * 8,
        out_shape=[jax.ShapeDtypeStruct((s, LANES), F32)] * 8,
        compiler_params=_cparams(("parallel",), 24),
        name="rope_tables",
    )(pos_b, consts)


def _rope128(x, cos, sin_signed):
    return x * cos + pltpu.roll(x, HEAD_DIM // 2, 1) * sin_signed


def _rope64(x, cos, sin_a, sin_b):
    return (x * cos + pltpu.roll(x, LANES - IDX_DIM // 2, 1) * sin_a
            + pltpu.roll(x, IDX_DIM // 2, 1) * sin_b)


PROJ_CHUNK = 2 * LANES


def _select_by_tile(j, values, dtype):
    out = jnp.asarray(values[-1], dtype)
    for t in range(len(values) - 2, -1, -1):
        out = jnp.where(j == t, jnp.asarray(values[t], dtype), out)
    return out


def _proj_kernel(a_ref, wt_ref, t0_ref, t1_ref, t2_ref, o_ref, wbf_ref, *, mode, scales, tn,
                 valid_rows, pad_heads):
    j = pl.program_id(0)

    @pl.when(pl.program_id(1) == 0)
    def _():
        if pad_heads:
            zeros = jnp.zeros((LANES - IDX_DIM, wbf_ref.shape[1]), BF16)
            for h in range(tn // LANES):
                wbf_ref[h * LANES:h * LANES + IDX_DIM, :] = (
                    wt_ref[h * IDX_DIM:(h + 1) * IDX_DIM, :].astype(BF16))
                wbf_ref[h * LANES + IDX_DIM:(h + 1) * LANES, :] = zeros
        else:
            w = wt_ref[...]
            if valid_rows < tn:
                w = jnp.where(lax.broadcasted_iota(I32, w.shape, 0) < valid_rows, w, 0.0)
            wbf_ref[...] = w.astype(BF16)

    scale = _select_by_tile(j, scales, F32)
    uniform = all(s == scales[0] for s in scales)
    cw = min(PROJ_CHUNK, tn)
    for c in range(tn // cw):
        acc = lax.dot_general(a_ref[...], wbf_ref[c * cw:(c + 1) * cw, :], (((1,), (1,)), ((), ())),
                              preferred_element_type=F32)
        for s in range(cw // LANES):
            x = acc[:, s * LANES:(s + 1) * LANES]
            if mode == ROPE128:
                x = _rope128(x, t0_ref[...], t1_ref[...])
            elif mode == ROPE64:
                x = _rope64(x, t0_ref[...], t1_ref[...], t2_ref[...])
            elif mode == SIGMOID:
                x = jax.nn.sigmoid(x)
            if not (uniform and scales[0] == 1.0):
                x = x * scale
            col = c * cw + s * LANES
            o_ref[:, col:col + LANES] = x.astype(o_ref.dtype)


def _proj(a, wt, row_offsets, tables, mode, scales, out_dtype, tm, tn, vmem_mb, name, *,
          valid_rows=None, pad_heads=False):
    m, k = a.shape
    assert wt.shape[1] == k and len(row_offsets) == len(scales) and tn % min(PROJ_CHUNK, tn) == 0
    src_rows = tn // 2 if pad_heads else tn
    valid_rows = tn if valid_rows is None else valid_rows
    assert all(o % SUBLANES == 0 and o + src_rows <= wt.shape[0] for o in row_offsets)
    tspec = pl.BlockSpec((tm, LANES), lambda j, i: (i, 0))
    return pl.pallas_call(
        functools.partial(_proj_kernel, mode=mode, scales=tuple(scales), tn=tn,
                          valid_rows=valid_rows, pad_heads=pad_heads),
        grid=(len(row_offsets), m // tm),
        in_specs=[pl.BlockSpec((tm, k), lambda j, i: (i, 0)),
                  pl.BlockSpec((pl.Element(src_rows), pl.Element(k)),
                               lambda j, i: (_select_by_tile(
                                   j, [o // SUBLANES for o in row_offsets], I32) * SUBLANES, 0)),
                  tspec, tspec, tspec],
        out_specs=pl.BlockSpec((tm, tn), lambda j, i: (i, j)),
        out_shape=jax.ShapeDtypeStruct((m, len(row_offsets) * tn), out_dtype),
        scratch_shapes=[pltpu.VMEM((tn, k), BF16)],
        compiler_params=_cparams(("arbitrary", "arbitrary"), vmem_mb),
        name=name,
    )(a, wt, *tables)


IDX_TQ = 256
IDX_KT = 512
IDX_PARTS = 4
IDX_BITS = 13
IDX_BISECT_PASSES = 13
IDX_EXTRACT_PASSES = 3
IDX_BRACKET_SLACK = 1.0 / 16


def _f32_key(x):
    bits = lax.bitcast_convert_type(x, I32)
    return bits ^ (lax.shift_right_arithmetic(bits, 31) & jnp.int32(0x7FFFFFFF))


def _key_f32(key):
    bits = key ^ (lax.shift_right_arithmetic(key, 31) & jnp.int32(0x7FFFFFFF))
    return lax.bitcast_convert_type(bits, F32)


def _indexer_kernel(ikw_ref, iq_ref, mask_ref, s_ref, gmax_ref, *, seq, k_top):
    b = pl.program_id(0)
    tq, kt = IDX_TQ, IDX_KT
    pr = kt // IDX_PARTS
    q0 = b * tq
    ntile = q0 // kt + 1

    w_t = jnp.transpose(ikw_ref[pl.ds(pl.multiple_of(q0, tq), tq), :])
    qidx = q0 + lax.broadcasted_iota(I32, (kt, tq), 1)
    gmax_ref[...] = jnp.full(gmax_ref.shape, -jnp.inf, F32)

    def score_tile(t, carry):
        r0 = pl.multiple_of(t * kt, kt)
        ik = ikw_ref[pl.ds(r0, kt), :].astype(BF16)
        acc = jnp.zeros((kt, tq), F32)
        for h in range(N_IDX_HEADS):
            qh = iq_ref[:, h * LANES:(h + 1) * LANES]
            s = lax.dot_general(ik, qh, (((1,), (1,)), ((), ())), preferred_element_type=F32)
            acc = acc + jnp.maximum(s, 0.0) * w_t[IDX_DIM + h:IDX_DIM + h + 1, :]
        kidx = r0 + lax.broadcasted_iota(I32, (kt, tq), 0)
        sc = jnp.where(kidx <= qidx, acc, -jnp.inf)
        s_ref[pl.ds(r0, kt), :] = sc
        gmax_ref[...] = functools.reduce(
            jnp.maximum, [sc[g * k_top:(g + 1) * k_top] for g in range(kt // k_top)], gmax_ref[...])
        return carry

    lax.fori_loop(0, ntile, score_tile, 0)

    def count(pred):
        def body(t, cnt):
            parts = []
            for part in range(IDX_PARTS):
                r0 = pl.multiple_of(t * kt + part * pr, pr)
                idx = r0 + lax.broadcasted_iota(I32, (pr, tq), 0)
                hit = pred(s_ref[pl.ds(r0, pr), :], idx)
                parts.append(jnp.sum(hit.reshape(pr // SUBLANES, SUBLANES, tq), axis=0))
            return cnt + ((parts[0] + parts[1]) + (parts[2] + parts[3]))
        cnt = lax.fori_loop(0, ntile, body, jnp.zeros((SUBLANES, tq), I32))
        return jnp.sum(cnt, axis=0, keepdims=True)

    g = gmax_ref[...]
    g_lo = jnp.min(g, axis=0, keepdims=True)
    g_hi = jnp.max(g, axis=0, keepdims=True)
    slack = jnp.maximum(jnp.abs(g_lo), jnp.abs(g_hi)) * IDX_BRACKET_SLACK
    lo0 = g_lo - slack
    hi0 = _key_f32(_f32_key(g_hi + slack) + 1)

    def reduce_max(value):
        def body(t, acc):
            parts = []
            for part in range(IDX_PARTS):
                r0 = pl.multiple_of(t * kt + part * pr, pr)
                x = value(s_ref[pl.ds(r0, pr), :])
                parts.append(jnp.max(x.reshape(pr // SUBLANES, SUBLANES, tq), axis=0))
            return jnp.maximum(acc, jnp.maximum(jnp.maximum(parts[0], parts[1]),
                                                jnp.maximum(parts[2], parts[3])))
        acc = lax.fori_loop(0, ntile, body, jnp.full((SUBLANES, tq), -jnp.inf, F32))
        return jnp.max(acc, axis=0, keepdims=True)

    def write_mask_tiles(select):
        def body(t, cnt):
            r0 = pl.multiple_of(t * kt, kt)
            idx = r0 + lax.broadcasted_iota(I32, (kt, tq), 0)
            sel = select(s_ref[pl.ds(r0, kt), :], idx)
            mask_ref[t] = jnp.transpose(sel).astype(mask_ref.dtype)
            return cnt + jnp.sum(sel.reshape(kt // SUBLANES, SUBLANES, tq), axis=0)
        cnt = lax.fori_loop(0, ntile, body, jnp.zeros((SUBLANES, tq), F32))
        return jnp.sum(cnt, axis=0, keepdims=True)

    def bisect_pass(p, c):
        lo, hi, c_hi, thr, done = c
        mid = lo + (hi - lo) * 0.5
        cnt = count(lambda sc, idx: jnp.where(sc >= mid, 1, 0))
        live = (1 - done) * jnp.where(mid > lo, jnp.where(mid < hi, 1, 0), 0)
        enough = jnp.where(cnt >= k_top, 1, 0)
        exact = live * jnp.where(cnt == k_top, 1, 0)
        up = live * enough * (1 - exact)
        down = live * (1 - enough)
        return (jnp.where(up > 0, mid, lo), jnp.where(down > 0, mid, hi),
                jnp.where(down > 0, cnt, c_hi), jnp.where(exact > 0, mid, thr),
                jnp.maximum(done, exact))

    zeros_i = jnp.zeros((1, tq), I32)
    _, cur, c_hi, thr, done = lax.fori_loop(
        0, IDX_BISECT_PASSES, bisect_pass, (lo0, hi0, zeros_i, jnp.zeros((1, tq), F32), zeros_i))
    need = k_top - c_hi
    for r in range(1, IDX_EXTRACT_PASSES + 1):
        below = cur
        cur = reduce_max(lambda sc: jnp.where(sc < below, sc, -jnp.inf))
        thr = jnp.where((1 - done) * jnp.where(need == r, 1, 0) > 0, cur, thr)
    selected = write_mask_tiles(lambda sc, idx: jnp.where(sc >= thr, 1.0, 0.0))

    def general_path():
        def search_pass(lo, hi):
            mid = (lo | hi) - lax.shift_right_arithmetic(lo ^ hi, 1)
            cand = _key_f32(mid)
            cnt = count(lambda sc, idx: jnp.where(sc >= cand, 1, 0))
            enough = cnt >= k_top
            return jnp.where(enough, mid, lo), jnp.where(enough, hi, mid - 1)

        def search_passes(carry):
            lo, hi = search_pass(*search_pass(*carry[:2]))
            return lo, hi, jnp.max(jnp.where(lo == hi, 0, 1))

        lo, _, _ = lax.while_loop(lambda c: c[2] > 0, search_passes,
                                  (_f32_key(lo0), _f32_key(hi0), jnp.int32(1)))
        kth = _key_f32(lo)
        cnt_gt = count(lambda sc, idx: jnp.where(sc > kth, 1, 0))
        cnt_eq = count(lambda sc, idx: jnp.where(sc == kth, 1, 0))
        want = k_top - cnt_gt
        finite = jnp.where(kth > -jnp.inf, 1, 0)
        tied = finite * jnp.where(cnt_eq > want, 1, 0)

        def tie_break():
            def index_pass(p, x):
                cand = x | lax.shift_left(jnp.int32(1), IDX_BITS - 1 - p)
                cnt = count(lambda sc, idx: jnp.where(sc == kth, jnp.where(idx < cand, 1, 0), 0))
                return jnp.where(cnt < want, cand, x)
            return lax.fori_loop(0, IDX_BITS, index_pass, jnp.zeros((1, tq), I32))

        last = lax.cond(jnp.max(tied) > 0, tie_break, lambda: jnp.full((1, tq), seq, I32))
        last = jnp.where(tied > 0, last, seq)
        eq_val = finite.astype(F32)
        write_mask_tiles(lambda sc, idx: jnp.where(
            sc > kth, 1.0, jnp.where(sc == kth, jnp.where(idx <= last, eq_val, 0.0), 0.0)))

    lax.cond(jnp.min(jnp.where(selected == k_top, 1, 0)) > 0, lambda: None, general_path)

    def clear_rest(t, carry):
        mask_ref[t] = jnp.zeros((tq, kt), mask_ref.dtype)
        return carry

    lax.fori_loop(ntile, seq // kt, clear_rest, 0)


def _indexer_mask(ikw, proj_a, iq_col_block, seq, k_top):
    return pl.pallas_call(
        functools.partial(_indexer_kernel, seq=seq, k_top=k_top),
        grid=(seq // IDX_TQ,),
        in_specs=[pl.BlockSpec((seq, LANES), lambda b: (0, 0)),
                  pl.BlockSpec((IDX_TQ, N_IDX_HEADS * LANES), lambda b: (b, iq_col_block))],
        out_specs=pl.BlockSpec((seq // IDX_KT, IDX_TQ, IDX_KT), lambda b: (0, b, 0)),
        out_shape=jax.ShapeDtypeStruct((seq // IDX_KT, seq, IDX_KT), BF16),
        scratch_shapes=[pltpu.VMEM((seq, IDX_TQ), F32), pltpu.VMEM((k_top, IDX_TQ), F32)],
        compiler_params=_cparams(("parallel",), 48),
        name="indexer_topk_mask",
    )(ikw, proj_a)


ATT_TQ = 512
ATT_KT = 512


def _attn_kernel(q_ref, k_ref, v_ref, mask_ref, o_ref, m_ref, acc_ref):
    i = pl.program_id(0)
    j = pl.program_id(1)

    @pl.when(j == 0)
    def _():
        m_ref[...] = jnp.full(m_ref.shape, NEG_BIG, F32)
        acc_ref[...] = jnp.zeros(acc_ref.shape, F32)

    @pl.when(j <= i)
    def _():
        bias = (1.0 - mask_ref[0].astype(F32)) * NEG_BIG
        for h in range(N_HEADS_ATTN):
            cs = slice(h * HEAD_DIM, (h + 1) * HEAD_DIM)
            s = lax.dot_general(q_ref[:, cs], k_ref[:, cs], (((1,), (1,)), ((), ())),
                                preferred_element_type=F32) + bias
            m_prev = m_ref[h]
            m_new = jnp.maximum(m_prev, jnp.max(s, axis=1, keepdims=True))
            alpha = jnp.exp2(m_prev - m_new)
            p = jnp.exp2(s - jnp.concatenate([m_new] * (s.shape[1] // LANES), axis=1))
            m_ref[h] = m_new
            v_aug = jnp.concatenate([v_ref[:, cs], jnp.ones((v_ref.shape[0], LANES), BF16)], axis=1)
            ca = slice(2 * h * HEAD_DIM, 2 * (h + 1) * HEAD_DIM)
            acc_ref[:, ca] = (jnp.concatenate([alpha, alpha], axis=1) * acc_ref[:, ca]
                              + jnp.dot(p.astype(BF16), v_aug, preferred_element_type=F32))

    @pl.when(j == i)
    def _():
        for h in range(N_HEADS_ATTN):
            cs = slice(h * HEAD_DIM, (h + 1) * HEAD_DIM)
            lo = 2 * h * HEAD_DIM
            o_ref[:, cs] = (acc_ref[:, lo:lo + HEAD_DIM]
                            / acc_ref[:, lo + HEAD_DIM:lo + 2 * HEAD_DIM]).astype(o_ref.dtype)


def _masked_attention(proj_qk, proj_v, mask, q_cb, k_cb, v_cb, seq):
    assert ATT_KT == IDX_KT and mask.shape == (seq // ATT_KT, seq, ATT_KT)
    tq, kt = ATT_TQ, ATT_KT
    return pl.pallas_call(
        _attn_kernel,
        grid=(seq // tq, seq // kt),
        in_specs=[pl.BlockSpec((tq, A_WIDTH), lambda i, j: (i, q_cb)),
                  pl.BlockSpec((kt, A_WIDTH), lambda i, j: (jnp.minimum(i, j), k_cb)),
                  pl.BlockSpec((kt, A_WIDTH), lambda i, j: (jnp.minimum(i, j), v_cb)),
                  pl.BlockSpec((1, tq, kt), lambda i, j: (jnp.minimum(i, j), i, 0))],
        out_specs=pl.BlockSpec((tq, A_WIDTH), lambda i, j: (i, 0)),
        out_shape=jax.ShapeDtypeStruct((seq, A_WIDTH), BF16),
        scratch_shapes=[pltpu.VMEM((N_HEADS_ATTN, tq, LANES), F32),
                        pltpu.VMEM((tq, 2 * A_WIDTH), F32)],
        compiler_params=_cparams(("parallel", "arbitrary"), 40),
        name="masked_attention",
    )(proj_qk, proj_qk, proj_v, mask)


def _retention_kernel(q_ref, k_ref, v_ref, g_ref, idec_ref, qdec_ref, kdec_ref, cdec_ref,
                      o_ref, state_ref):
    @pl.when(pl.program_id(0) == 0)
    def _():
        state_ref[...] = jnp.zeros(state_ref.shape, F32)

    for h in range(N_RET_HEADS):
        ck = slice(h * RET_QK_DIM, (h + 1) * RET_QK_DIM)
        cv = slice(h * RET_V_DIM, (h + 1) * RET_V_DIM)
        q = q_ref[:, ck]
        k = k_ref[:, ck]
        v = v_ref[:, cv]
        scores = lax.dot_general(q, k, (((1,), (1,)), ((), ())),
                                 preferred_element_type=F32) * idec_ref[h]
        inner = jnp.dot(scores.astype(BF16), v, preferred_element_type=F32)
        state = state_ref[h]
        cross = jnp.dot(q, state.astype(BF16), preferred_element_type=F32) * qdec_ref[h]
        y = inner + cross
        mu = jnp.mean(y, axis=-1, keepdims=True)
        var = jnp.mean(jnp.square(y - mu), axis=-1, keepdims=True)
        yn = (y - mu) * lax.rsqrt(var + EPS)
        g = g_ref[:, cv]
        o_ref[:, cv] = (g * jax.nn.sigmoid(g) * yn).astype(o_ref.dtype)
        k_dec_t = jnp.transpose(k.astype(F32) * kdec_ref[h]).astype(BF16)
        state_ref[h] = state * cdec_ref[h] + jnp.dot(k_dec_t, v, preferred_element_type=F32)


def _retention(proj_qk, proj_v, proj_g, q_cb, k_cb, v_cb, g_cb, seq):
    c = RET_CHUNK
    h = N_RET_HEADS
    log_gamma = jnp.log1p(-jnp.exp2(-5.0 - jnp.arange(h, dtype=F32)))
    i = jnp.arange(c, dtype=F32)
    diff = i[:, None] - i[None, :]
    idec = jnp.where(diff[None] >= 0,
                     jnp.exp(jnp.maximum(diff, 0.0)[None] * log_gamma[:, None, None]), 0.0)
    kdec = jnp.exp((c - 1 - i)[None, :] * log_gamma[:, None])
    qdec = jnp.exp((i + 1)[None, :] * log_gamma[:, None])
    cdec = jnp.exp(c * log_gamma)
    qdec_b = jnp.broadcast_to(qdec[:, :, None], (h, c, RET_V_DIM))
    kdec_b = jnp.broadcast_to(kdec[:, :, None], (h, c, RET_QK_DIM))
    cdec_b = jnp.broadcast_to(cdec[:, None, None], (h, RET_QK_DIM, RET_V_DIM))
    const = lambda shape: pl.BlockSpec(shape, lambda n: (0, 0, 0))
    return pl.pallas_call(
        _retention_kernel,
        grid=(seq // c,),
        in_specs=[pl.BlockSpec((c, RET_QK_WIDTH), lambda n: (n, q_cb)),
                  pl.BlockSpec((c, RET_QK_WIDTH), lambda n: (n, k_cb)),
                  pl.BlockSpec((c, RET_V_WIDTH), lambda n: (n, v_cb)),
                  pl.BlockSpec((c, RET_V_WIDTH), lambda n: (n, g_cb)),
                  const((h, c, c)), const((h, c, RET_V_DIM)), const((h, c, RET_QK_DIM)),
                  const((h, RET_QK_DIM, RET_V_DIM))],
        out_specs=pl.BlockSpec((c, RET_V_WIDTH), lambda n: (n, 0)),
        out_shape=jax.ShapeDtypeStruct((seq, RET_V_WIDTH), BF16),
        scratch_shapes=[pltpu.VMEM((h, RET_QK_DIM, RET_V_DIM), F32)],
        compiler_params=_cparams(("arbitrary",), 24),
        name="retention",
    )(proj_qk, proj_qk, proj_v, proj_g, idec, qdec_b, kdec_b, cdec_b)


def _mem_attn_kernel(q_ref, kv_ref, o_ref):
    for h in range(N_MEM_HEADS):
        cs = slice(h * MEM_HEAD_DIM, (h + 1) * MEM_HEAD_DIM)
        vs = slice(MEM_WIDTH + h * MEM_HEAD_DIM, MEM_WIDTH + (h + 1) * MEM_HEAD_DIM)
        s = lax.dot_general(q_ref[:, cs], kv_ref[:, cs], (((1,), (1,)), ((), ())),
                            preferred_element_type=F32)
        e = jnp.exp(s - jnp.max(s, axis=-1, keepdims=True))
        p = e / jnp.sum(e, axis=-1, keepdims=True)
        o_ref[:, cs] = jnp.dot(p.astype(BF16), kv_ref[:, vs],
                               preferred_element_type=F32).astype(o_ref.dtype)


def _mem_attention(proj_a, mem_kv, q_cb, seq, tm):
    n_mem = mem_kv.shape[0]
    return pl.pallas_call(
        _mem_attn_kernel,
        grid=(seq // tm,),
        in_specs=[pl.BlockSpec((tm, MEM_WIDTH), lambda i: (i, q_cb)),
                  pl.BlockSpec((n_mem, 2 * MEM_WIDTH), lambda i: (0, 0))],
        out_specs=pl.BlockSpec((tm, MEM_WIDTH), lambda i: (i, 0)),
        out_shape=jax.ShapeDtypeStruct((seq, MEM_WIDTH), BF16),
        compiler_params=_cparams(("parallel",), 24),
        name="memory_attention",
    )(proj_a, mem_kv)


def _merge_kernel(oa_ref, ob_ref, oc_ref, wa_ref, wb_ref, wc_ref, g0_ref, g1_ref, g2_ref, o_ref):
    for c in range(0, o_ref.shape[1], PROJ_CHUNK):
        cs = slice(c, c + PROJ_CHUNK)
        mixed = g0_ref[:, cs] * jnp.dot(oa_ref[...], wa_ref[:, cs], preferred_element_type=F32)
        mixed = mixed + g1_ref[:, cs] * jnp.dot(ob_ref[...], wb_ref[:, cs], preferred_element_type=F32)
        mixed = mixed + g2_ref[:, cs] * jnp.dot(oc_ref[...], wc_ref[:, cs], preferred_element_type=F32)
        o_ref[:, cs] = mixed.astype(o_ref.dtype)


def _merge(o_a, o_b, o_c, w_a, w_b, w_c, proj_b, gate_cb0, tm, tn):
    seq = o_a.shape[0]
    d = w_a.shape[1]
    per_gate = d // tn
    act = lambda w: pl.BlockSpec((tm, w), lambda i, j: (i, 0))
    wgt = lambda k: pl.BlockSpec((k, tn), lambda i, j: (0, j))
    gate = lambda g: pl.BlockSpec((tm, tn), lambda i, j: (i, gate_cb0 + g * per_gate + j))
    return pl.pallas_call(
        _merge_kernel,
        grid=(seq // tm, d // tn),
        in_specs=[act(o_a.shape[1]), act(o_b.shape[1]), act(o_c.shape[1]),
                  wgt(w_a.shape[0]), wgt(w_b.shape[0]), wgt(w_c.shape[0]),
                  gate(0), gate(1), gate(2)],
        out_specs=pl.BlockSpec((tm, tn), lambda i, j: (i, j)),
        out_shape=jax.ShapeDtypeStruct((seq, d), BF16),
        compiler_params=_cparams(("parallel", "arbitrary"), 48),
        name="gated_merge",
    )(o_a, o_b, o_c, w_a, w_b, w_c, proj_b, proj_b, proj_b)


def _mm_norm_res_kernel(a_ref, w_ref, x_ref, g_ref, *rest, with_next_norm):
    if with_next_norm:
        g2_ref, o_ref, h_ref = rest
    else:
        (o_ref,) = rest
    d = w_ref.shape[1]
    cols = [slice(c, c + PROJ_CHUNK) for c in range(0, d, PROJ_CHUNK)]
    accs = [jnp.dot(a_ref[...], w_ref[:, cs], preferred_element_type=F32) for cs in cols]
    ssq = sum(jnp.sum(acc * acc, axis=-1, keepdims=True) for acc in accs)
    inv = lax.rsqrt(ssq / d + EPS)
    x_new = [x_ref[:, cs] + acc * inv * g_ref[:, cs] for cs, acc in zip(cols, accs)]
    for cs, xc in zip(cols, x_new):
        o_ref[:, cs] = xc
    if with_next_norm:
        ssq2 = sum(jnp.sum(xc * xc, axis=-1, keepdims=True) for xc in x_new)
        inv2 = lax.rsqrt(ssq2 / d + EPS)
        for cs, xc in zip(cols, x_new):
            h_ref[:, cs] = (xc * inv2 * g2_ref[:, cs]).astype(h_ref.dtype)


def _mm_norm_res(a, w, x, g, g2, tm, vmem_mb, name):
    m, k = a.shape
    d = w.shape[1]
    row = lambda width: pl.BlockSpec((tm, width), lambda i: (i, 0))
    vec = pl.BlockSpec((1, d), lambda i: (0, 0))
    with_next_norm = g2 is not None
    in_specs = [row(k), pl.BlockSpec((k, d), lambda i: (0, 0), pipeline_mode=pl.Buffered(1)),
                row(d), vec]
    args = [a, w, x, g.reshape(1, d)]
    out_specs = [row(d)]
    out_shape = [jax.ShapeDtypeStruct((m, d), F32)]
    if with_next_norm:
        in_specs.append(vec)
        args.append(g2.reshape(1, d))
        out_specs.append(row(d))
        out_shape.append(jax.ShapeDtypeStruct((m, d), BF16))
    return pl.pallas_call(
        functools.partial(_mm_norm_res_kernel, with_next_norm=with_next_norm),
        grid=(m // tm,),
        in_specs=in_specs,
        out_specs=out_specs,
        out_shape=out_shape,
        compiler_params=_cparams(("parallel",), vmem_mb),
        name=name,
    )(*args)


def _swiglu_kernel(a_ref, wg_ref, wu_ref, o_ref, wg_bf_ref, wu_bf_ref):
    @pl.when(pl.program_id(1) == 0)
    def _():
        wg_bf_ref[...] = wg_ref[...].astype(BF16)
        wu_bf_ref[...] = wu_ref[...].astype(BF16)

    gate = jnp.dot(a_ref[...], wg_bf_ref[...], preferred_element_type=F32)
    up = jnp.dot(a_ref[...], wu_bf_ref[...], preferred_element_type=F32)
    o_ref[...] = (gate * jax.nn.sigmoid(gate) * up).astype(o_ref.dtype)


def _swiglu(a, w, tm, tn):
    m, k = a.shape
    d_ff = w.shape[1] // 2
    nt = d_ff // tn
    return pl.pallas_call(
        _swiglu_kernel,
        grid=(nt, m // tm),
        in_specs=[pl.BlockSpec((tm, k), lambda j, i: (i, 0)),
                  pl.BlockSpec((k, tn), lambda j, i: (0, j)),
                  pl.BlockSpec((k, tn), lambda j, i: (0, nt + j))],
        out_specs=pl.BlockSpec((tm, tn), lambda j, i: (i, j)),
        out_shape=jax.ShapeDtypeStruct((m, d_ff), BF16),
        scratch_shapes=[pltpu.VMEM((k, tn), BF16), pltpu.VMEM((k, tn), BF16)],
        compiler_params=_cparams(("arbitrary", "arbitrary"), 48),
        name="swiglu_in",
    )(a, w, w)


def _layer(x, mem, positions, g_pre_mix, g_mem, w_in, w_mem_kv, w_branch_a, w_branch_b,
           w_branch_c, w_out, g_post_mix, g_pre_ffn, w_ffn_in, w_ffn_out, g_post_ffn):
    seq, d = x.shape
    k_top = min(TOPK_MAX, seq // 4)

    sizes = (A_WIDTH, A_WIDTH, A_WIDTH, IDX_Q_WIDTH, IDX_DIM, N_IDX_HEADS,
             RET_QK_WIDTH, RET_QK_WIDTH, RET_V_WIDTH, RET_V_WIDTH, MEM_WIDTH, 3 * d)
    (o_aq, o_ak, o_av, o_iq, o_ik, o_iw, o_rq, o_rk, o_rv, o_rg, o_mq, o_gate) = (
        int(o) for o in np.cumsum((0,) + sizes[:-1]))
    w_in_t = jnp.swapaxes(w_in, 0, 1)

    tabs = _rope_tables(positions, 1024)
    (c128, s128, c64, sa64, sb64, ck, sak, sbk) = tabs
    t128 = (c128, s128, s128)
    t64 = (c64, sa64, sb64)

    h = _rmsnorm(x, g_pre_mix, 512)

    tn = 1024
    proj_iq = _proj(h, w_in_t, [o_iq, o_iq + tn // 2], t64, ROPE64, [IDX_DIM ** -0.5] * 2, BF16,
                    1024, tn, 48, "proj_index_q", pad_heads=True)
    proj_r = _proj(h, w_in_t, [o_aq, o_ak, o_rq, o_rk], t128, ROPE128,
                   [HEAD_DIM ** -0.5 * LOG2E, 1.0, 1.0, RET_QK_DIM ** -0.5], BF16, 1024, tn, 48,
                   "proj_rope")
    proj_p = _proj(h, w_in_t, [o_rv, o_rv + tn, o_av, o_mq], t128, PLAIN,
                   [1.0, 1.0, 1.0, MEM_HEAD_DIM ** -0.5], BF16, 1024, tn, 48, "proj_plain")
    gates = _proj(h, w_in_t, [o_gate + t * tn for t in range(3 * d // tn)], t128, SIGMOID,
                  [1.0] * (3 * d // tn), F32, 1024, tn, 52, "proj_gates")
    r_g = _proj(h, w_in_t, [o_rg, o_rg + tn], t128, PLAIN, [1.0] * 2, F32, 1024, tn, 52,
                "proj_ret_gate")
    assert o_iw == o_ik + IDX_DIM
    ikw = _proj(h, w_in_t, [o_ik], (ck, sak, sbk), ROPE64, [1.0], F32, 1024, LANES, 24,
                "proj_index_kw", valid_rows=IDX_DIM + N_IDX_HEADS)

    mask = _indexer_mask(ikw, proj_iq, 0, seq, k_top)
    o_a = _masked_attention(proj_r, proj_p, mask, 0, 1, 2, seq)

    o_b = _retention(proj_r, proj_p, r_g, 2, 3, 0, 0, seq)

    mem_n = _rmsnorm(mem, g_mem, mem.shape[0])
    mem_kv = _proj(mem_n, jnp.swapaxes(w_mem_kv, 0, 1), [0, tn], t128, PLAIN, [1.0] * 2, BF16,
                   mem.shape[0], tn, 40, "proj_mem_kv")
    o_c = _mem_attention(proj_p, mem_kv, 3, seq, 512)

    mixed = _merge(o_a, o_b, o_c, w_branch_a.astype(BF16), w_branch_b.astype(BF16),
                   w_branch_c.astype(BF16), gates, 0, 512, tn)
    x1, h2 = _mm_norm_res(mixed, w_out.astype(BF16), x, g_post_mix, g_pre_ffn, 512, 40,
                          "out_proj_norm_res")

    act = _swiglu(h2, w_ffn_in, 1024, 512)
    (x2,) = _mm_norm_res(act, w_ffn_out.astype(BF16), x1, g_post_ffn, None, 256, 48,
                         "ffn_out_norm_res")
    return x2


def kernel(x, mem, positions, g_pre_mix, g_mem, w_in, w_mem_kv, w_branch_a, w_branch_b, w_branch_c,
           w_out, g_post_mix, g_pre_ffn, w_ffn_in, w_ffn_out, g_post_ffn):
    assert x.shape[0] == 1 and g_pre_mix.shape[0] == 1
    out = _layer(x[0], mem[0], positions[0], g_pre_mix[0], g_mem[0], w_in[0], w_mem_kv[0],
                 w_branch_a[0], w_branch_b[0], w_branch_c[0], w_out[0], g_post_mix[0],
                 g_pre_ffn[0], w_ffn_in[0], w_ffn_out[0], g_post_ffn[0])
    return out[None]
```

```python
import functools

import jax
import jax.numpy as jnp
import numpy as np
from jax import lax
from jax.experimental import pallas as pl
from jax.experimental.pallas import tpu as pltpu

D_MODEL = 2048
N_HEADS_ATTN = 8
HEAD_DIM = 128
N_IDX_HEADS = 16
IDX_DIM = 64
TOPK_MAX = 256
N_RET_HEADS = 8
RET_QK_DIM = 128
RET_V_DIM = 256
RET_CHUNK = 128
N_MEM_HEADS = 4
MEM_HEAD_DIM = 256
ROPE_THETA = 10000.0
EPS = 1e-6

A_WIDTH = N_HEADS_ATTN * HEAD_DIM
IDX_Q_WIDTH = N_IDX_HEADS * IDX_DIM
RET_QK_WIDTH = N_RET_HEADS * RET_QK_DIM
RET_V_WIDTH = N_RET_HEADS * RET_V_DIM
MEM_WIDTH = N_MEM_HEADS * MEM_HEAD_DIM

LANES = 128
SUBLANES = 8
VMEM_BYTES_V7X = 64 * 1024 * 1024
INT_MIN = -(2 ** 31)
NEG_BIG = -1e30
LOG2E = 1.4426950408889634

F32 = jnp.float32
BF16 = jnp.bfloat16
I32 = jnp.int32

PLAIN, ROPE128, ROPE64, SIGMOID = 0, 1, 2, 3


def _cparams(sem, vmem_mb):
    assert vmem_mb * 1024 * 1024 < VMEM_BYTES_V7X
    return pltpu.CompilerParams(dimension_semantics=sem, vmem_limit_bytes=vmem_mb * 1024 * 1024)


def _rms(x, g):
    return x * lax.rsqrt(jnp.mean(x * x, axis=-1, keepdims=True) + EPS) * g


def _rmsnorm_kernel(x_ref, g_ref, o_ref):
    o_ref[...] = _rms(x_ref[...], g_ref[...]).astype(o_ref.dtype)


def _rmsnorm(x, g, tm):
    n, d = x.shape
    return pl.pallas_call(
        _rmsnorm_kernel,
        grid=(n // tm,),
        in_specs=[pl.BlockSpec((tm, d), lambda i: (i, 0)), pl.BlockSpec((1, d), lambda i: (0, 0))],
        out_specs=pl.BlockSpec((tm, d), lambda i: (i, 0)),
        out_shape=jax.ShapeDtypeStruct((n, d), BF16),
        compiler_params=_cparams(("parallel",), 24),
        name="rmsnorm",
    )(x, g.reshape(1, d))


def _rope_table_kernel(pos_ref, c_ref, c128_ref, s128_ref, c64_ref, sa64_ref, sb64_ref,
                       ck_ref, sak_ref, sbk_ref):
    pos = pos_ref[...]
    ang128 = pos * c_ref[0:1, :]
    ang64 = pos * c_ref[1:2, :]
    c128_ref[...] = jnp.cos(ang128)
    s128_ref[...] = jnp.sin(ang128) * c_ref[2:3, :]
    cos64 = jnp.cos(ang64)
    sin64 = jnp.sin(ang64)
    sa = sin64 * c_ref[3:4, :]
    sb = sin64 * c_ref[4:5, :]
    c64_ref[...] = cos64
    sa64_ref[...] = sa
    sb64_ref[...] = sb
    ck_ref[...] = cos64 * c_ref[5:6, :] + c_ref[6:7, :]
    sak_ref[...] = sa * c_ref[5:6, :]
    sbk_ref[...] = sb * c_ref[5:6, :]


def _rope_tables(positions, tm):
    s = positions.shape[-1]
    pos_b = jnp.broadcast_to(positions.reshape(s, 1).astype(F32), (s, LANES))
    lane = np.arange(LANES)
    f128 = ROPE_THETA ** (-jnp.arange(0, HEAD_DIM, 2, dtype=F32) / HEAD_DIM)
    f64 = ROPE_THETA ** (-jnp.arange(0, IDX_DIM, 2, dtype=F32) / IDX_DIM)
    half64 = (lane % IDX_DIM) < IDX_DIM // 2
    rows = [
        jnp.tile(f128, 2),
        jnp.tile(f64, 4),
        jnp.asarray(np.where(lane < HEAD_DIM // 2, -1.0, 1.0), F32),
        jnp.asarray(np.where(half64, -1.0, 0.0), F32),
        jnp.asarray(np.where(half64, 0.0, 1.0), F32),
        jnp.asarray(np.where(lane < IDX_DIM, 1.0, 0.0), F32),
        jnp.asarray(np.where(lane < IDX_DIM, 0.0,
                             np.where(lane < IDX_DIM + N_IDX_HEADS, N_IDX_HEADS ** -0.5, 1.0)), F32),
        jnp.zeros((LANES,), F32),
    ]
    consts = jnp.stack(rows)
    spec = pl.BlockSpec((tm, LANES), lambda i: (i, 0))
    return pl.pallas_call(
        _rope_table_kernel,
        grid=(s // tm,),
        in_specs=[spec, pl.BlockSpec((SUBLANES, LANES), lambda i: (0, 0))],
        out_specs=[spec] * 8,
        out_shape=[jax.ShapeDtypeStruct((s, LANES), F32)] * 8,
        compiler_params=_cparams(("parallel",), 24),
        name="rope_tables",
    )(pos_b, consts)


def _rope128(x, cos, sin_signed):
    return x * cos + pltpu.roll(x, HEAD_DIM // 2, 1) * sin_signed


def _rope64(x, cos, sin_a, sin_b):
    return (x * cos + pltpu.roll(x, LANES - IDX_DIM // 2, 1) * sin_a
            + pltpu.roll(x, IDX_DIM // 2, 1) * sin_b)


PROJ_CHUNK = 2 * LANES


def _select_by_tile(j, values, dtype):
    out = jnp.asarray(values[-1], dtype)
    for t in range(len(values) - 2, -1, -1):
        out = jnp.where(j == t, jnp.asarray(values[t], dtype), out)
    return out


def _proj_kernel(a_ref, wt_ref, t0_ref, t1_ref, t2_ref, o_ref, wbf_ref, *, mode, scales, tn,
                 valid_rows, pad_heads):
    j = pl.program_id(0)

    @pl.when(pl.program_id(1) == 0)
    def _():
        if pad_heads:
            zeros = jnp.zeros((LANES - IDX_DIM, wbf_ref.shape[1]), BF16)
            for h in range(tn // LANES):
                wbf_ref[h * LANES:h * LANES + IDX_DIM, :] = (
                    wt_ref[h * IDX_DIM:(h + 1) * IDX_DIM, :].astype(BF16))
                wbf_ref[h * LANES + IDX_DIM:(h + 1) * LANES, :] = zeros
        else:
            w = wt_ref[...]
            if valid_rows < tn:
                w = jnp.where(lax.broadcasted_iota(I32, w.shape, 0) < valid_rows, w, 0.0)
            wbf_ref[...] = w.astype(BF16)

    scale = _select_by_tile(j, scales, F32)
    uniform = all(s == scales[0] for s in scales)
    cw = min(PROJ_CHUNK, tn)
    for c in range(tn // cw):
        acc = lax.dot_general(a_ref[...], wbf_ref[c * cw:(c + 1) * cw, :], (((1,), (1,)), ((), ())),
                              preferred_element_type=F32)
        for s in range(cw // LANES):
            x = acc[:, s * LANES:(s + 1) * LANES]
            if mode == ROPE128:
                x = _rope128(x, t0_ref[...], t1_ref[...])
            elif mode == ROPE64:
                x = _rope64(x, t0_ref[...], t1_ref[...], t2_ref[...])
            elif mode == SIGMOID:
                x = jax.nn.sigmoid(x)
            if not (uniform and scales[0] == 1.0):
                x = x * scale
            col = c * cw + s * LANES
            o_ref[:, col:col + LANES] = x.astype(o_ref.dtype)


def _proj(a, wt, row_offsets, tables, mode, scales, out_dtype, tm, tn, vmem_mb, name, *,
          valid_rows=None, pad_heads=False):
    m, k = a.shape
    assert wt.shape[1] == k and len(row_offsets) == len(scales) and tn % min(PROJ_CHUNK, tn) == 0
    src_rows = tn // 2 if pad_heads else tn
    valid_rows = tn if valid_rows is None else valid_rows
    assert all(o % SUBLANES == 0 and o + src_rows <= wt.shape[0] for o in row_offsets)
    tspec = pl.BlockSpec((tm, LANES), lambda j, i: (i, 0))
    return pl.pallas_call(
        functools.partial(_proj_kernel, mode=mode, scales=tuple(scales), tn=tn,
                          valid_rows=valid_rows, pad_heads=pad_heads),
        grid=(len(row_offsets), m // tm),
        in_specs=[pl.BlockSpec((tm, k), lambda j, i: (i, 0)),
                  pl.BlockSpec((pl.Element(src_rows), pl.Element(k)),
                               lambda j, i: (_select_by_tile(
                                   j, [o // SUBLANES for o in row_offsets], I32) * SUBLANES, 0)),
                  tspec, tspec, tspec],
        out_specs=pl.BlockSpec((tm, tn), lambda j, i: (i, j)),
        out_shape=jax.ShapeDtypeStruct((m, len(row_offsets) * tn), out_dtype),
        scratch_shapes=[pltpu.VMEM((tn, k), BF16)],
        compiler_params=_cparams(("arbitrary", "arbitrary"), vmem_mb),
        name=name,
    )(a, wt, *tables)


IDX_TQ = 256
IDX_KT = 512
IDX_PARTS = 4
IDX_BITS = 13
IDX_BISECT_PASSES = 13
IDX_EXTRACT_PASSES = 3
IDX_BRACKET_SLACK = 1.0 / 16


def _f32_key(x):
    bits = lax.bitcast_convert_type(x, I32)
    return bits ^ (lax.shift_right_arithmetic(bits, 31) & jnp.int32(0x7FFFFFFF))


def _key_f32(key):
    bits = key ^ (lax.shift_right_arithmetic(key, 31) & jnp.int32(0x7FFFFFFF))
    return lax.bitcast_convert_type(bits, F32)


def _indexer_kernel(ikw_ref, iq_ref, mask_ref, s_ref, gmax_ref, *, seq, k_top):
    b = pl.program_id(0)
    tq, kt = IDX_TQ, IDX_KT
    pr = kt // IDX_PARTS
    q0 = b * tq
    ntile = q0 // kt + 1

    w_t = jnp.transpose(ikw_ref[pl.ds(pl.multiple_of(q0, tq), tq), :])
    qidx = q0 + lax.broadcasted_iota(I32, (kt, tq), 1)
    gmax_ref[...] = jnp.full(gmax_ref.shape, -jnp.inf, F32)

    def score_tile(t, carry):
        r0 = pl.multiple_of(t * kt, kt)
        ik = ikw_ref[pl.ds(r0, kt), :].astype(BF16)
        acc = jnp.zeros((kt, tq), F32)
        for h in range(N_IDX_HEADS):
            qh = iq_ref[:, h * LANES:(h + 1) * LANES]
            s = lax.dot_general(ik, qh, (((1,), (1,)), ((), ())), preferred_element_type=F32)
            acc = acc + jnp.maximum(s, 0.0) * w_t[IDX_DIM + h:IDX_DIM + h + 1, :]
        kidx = r0 + lax.broadcasted_iota(I32, (kt, tq), 0)
        sc = jnp.where(kidx <= qidx, acc, -jnp.inf)
        s_ref[pl.ds(r0, kt), :] = sc
        gmax_ref[...] = functools.reduce(
            jnp.maximum, [sc[g * k_top:(g + 1) * k_top] for g in range(kt // k_top)], gmax_ref[...])
        return carry

    lax.fori_loop(0, ntile, score_tile, 0)

    def count(pred):
        def body(t, cnt):
            parts = []
            for part in range(IDX_PARTS):
                r0 = pl.multiple_of(t * kt + part * pr, pr)
                idx = r0 + lax.broadcasted_iota(I32, (pr, tq), 0)
                hit = pred(s_ref[pl.ds(r0, pr), :], idx)
                parts.append(jnp.sum(hit.reshape(pr // SUBLANES, SUBLANES, tq), axis=0))
            return cnt + ((parts[0] + parts[1]) + (parts[2] + parts[3]))
        cnt = lax.fori_loop(0, ntile, body, jnp.zeros((SUBLANES, tq), I32))
        return jnp.sum(cnt, axis=0, keepdims=True)

    g = gmax_ref[...]
    g_lo = jnp.min(g, axis=0, keepdims=True)
    g_hi = jnp.max(g, axis=0, keepdims=True)
    slack = jnp.maximum(jnp.abs(g_lo), jnp.abs(g_hi)) * IDX_BRACKET_SLACK
    lo0 = g_lo - slack
    hi0 = _key_f32(_f32_key(g_hi + slack) + 1)

    def reduce_max(value):
        def body(t, acc):
            parts = []
            for part in range(IDX_PARTS):
                r0 = pl.multiple_of(t * kt + part * pr, pr)
                x = value(s_ref[pl.ds(r0, pr), :])
                parts.append(jnp.max(x.reshape(pr // SUBLANES, SUBLANES, tq), axis=0))
            return jnp.maximum(acc, jnp.maximum(jnp.maximum(parts[0], parts[1]),
                                                jnp.maximum(parts[2], parts[3])))
        acc = lax.fori_loop(0, ntile, body, jnp.full((SUBLANES, tq), -jnp.inf, F32))
        return jnp.max(acc, axis=0, keepdims=True)

    def write_mask_tiles(select):
        def body(t, cnt):
            r0 = pl.multiple_of(t * kt, kt)
            idx = r0 + lax.broadcasted_iota(I32, (kt, tq), 0)
            sel = select(s_ref[pl.ds(r0, kt), :], idx)
            mask_ref[t] = jnp.transpose(sel).astype(mask_ref.dtype)
            return cnt + jnp.sum(sel.reshape(kt // SUBLANES, SUBLANES, tq), axis=0)
        cnt = lax.fori_loop(0, ntile, body, jnp.zeros((SUBLANES, tq), F32))
        return jnp.sum(cnt, axis=0, keepdims=True)

    def bisect_pass(p, c):
        lo, hi, c_hi, thr, done = c
        mid = lo + (hi - lo) * 0.5
        cnt = count(lambda sc, idx: jnp.where(sc >= mid, 1, 0))
        live = (1 - done) * jnp.where(mid > lo, jnp.where(mid < hi, 1, 0), 0)
        enough = jnp.where(cnt >= k_top, 1, 0)
        exact = live * jnp.where(cnt == k_top, 1, 0)
        up = live * enough * (1 - exact)
        down = live * (1 - enough)
        return (jnp.where(up > 0, mid, lo), jnp.where(down > 0, mid, hi),
                jnp.where(down > 0, cnt, c_hi), jnp.where(exact > 0, mid, thr),
                jnp.maximum(done, exact))

    zeros_i = jnp.zeros((1, tq), I32)
    _, cur, c_hi, thr, done = lax.fori_loop(
        0, IDX_BISECT_PASSES, bisect_pass, (lo0, hi0, zeros_i, jnp.zeros((1, tq), F32), zeros_i))
    need = k_top - c_hi
    for r in range(1, IDX_EXTRACT_PASSES + 1):
        below = cur
        cur = reduce_max(lambda sc: jnp.where(sc < below, sc, -jnp.inf))
        thr = jnp.where((1 - done) * jnp.where(need == r, 1, 0) > 0, cur, thr)
    selected = write_mask_tiles(lambda sc, idx: jnp.where(sc >= thr, 1.0, 0.0))

    def general_path():
        def search_pass(lo, hi):
            mid = (lo | hi) - lax.shift_right_arithmetic(lo ^ hi, 1)
            cand = _key_f32(mid)
            cnt = count(lambda sc, idx: jnp.where(sc >= cand, 1, 0))
            enough = cnt >= k_top
            return jnp.where(enough, mid, lo), jnp.where(enough, hi, mid - 1)

        def search_passes(carry):
            lo, hi = search_pass(*search_pass(*carry[:2]))
            return lo, hi, jnp.max(jnp.where(lo == hi, 0, 1))

        lo, _, _ = lax.while_loop(lambda c: c[2] > 0, search_passes,
                                  (_f32_key(lo0), _f32_key(hi0), jnp.int32(1)))
        kth = _key_f32(lo)
        cnt_gt = count(lambda sc, idx: jnp.where(sc > kth, 1, 0))
        cnt_eq = count(lambda sc, idx: jnp.where(sc == kth, 1, 0))
        want = k_top - cnt_gt
        finite = jnp.where(kth > -jnp.inf, 1, 0)
        tied = finite * jnp.where(cnt_eq > want, 1, 0)

        def tie_break():
            def index_pass(p, x):
                cand = x | lax.shift_left(jnp.int32(1), IDX_BITS - 1 - p)
                cnt = count(lambda sc, idx: jnp.where(sc == kth, jnp.where(idx < cand, 1, 0), 0))
                return jnp.where(cnt < want, cand, x)
            return lax.fori_loop(0, IDX_BITS, index_pass, jnp.zeros((1, tq), I32))

        last = lax.cond(jnp.max(tied) > 0, tie_break, lambda: jnp.full((1, tq), seq, I32))
        last = jnp.where(tied > 0, last, seq)
        eq_val = finite.astype(F32)
        write_mask_tiles(lambda sc, idx: jnp.where(
            sc > kth, 1.0, jnp.where(sc == kth, jnp.where(idx <= last, eq_val, 0.0), 0.0)))

    lax.cond(jnp.min(jnp.where(selected == k_top, 1, 0)) > 0, lambda: None, general_path)

    def clear_rest(t, carry):
        mask_ref[t] = jnp.zeros((tq, kt), mask_ref.dtype)
        return carry

    lax.fori_loop(ntile, seq // kt, clear_rest, 0)


def _indexer_mask(ikw, proj_a, iq_col_block, seq, k_top):
    return pl.pallas_call(
        functools.partial(_indexer_kernel, seq=seq, k_top=k_top),
        grid=(seq // IDX_TQ,),
        in_specs=[pl.BlockSpec((seq, LANES), lambda b: (0, 0)),
                  pl.BlockSpec((IDX_TQ, N_IDX_HEADS * LANES), lambda b: (b, iq_col_block))],
        out_specs=pl.BlockSpec((seq // IDX_KT, IDX_TQ, IDX_KT), lambda b: (0, b, 0)),
        out_shape=jax.ShapeDtypeStruct((seq // IDX_KT, seq, IDX_KT), BF16),
        scratch_shapes=[pltpu.VMEM((seq, IDX_TQ), F32), pltpu.VMEM((k_top, IDX_TQ), F32)],
        compiler_params=_cparams(("parallel",), 48),
        name="indexer_topk_mask",
    )(ikw, proj_a)


ATT_TQ = 1024
ATT_KT = 512


def _attn_kernel(q_ref, k_ref, v_ref, mask_ref, o_ref, m_ref, acc_ref):
    i = pl.program_id(0)
    j = pl.program_id(1)

    @pl.when(j == 0)
    def _():
        m_ref[...] = jnp.full(m_ref.shape, NEG_BIG, F32)
        acc_ref[...] = jnp.zeros(acc_ref.shape, F32)

    last = (i + 1) * (ATT_TQ // ATT_KT) - 1

    @pl.when(j <= last)
    def _():
        bias = (1.0 - mask_ref[0].astype(F32)) * NEG_BIG
        for h in range(N_HEADS_ATTN):
            cs = slice(h * HEAD_DIM, (h + 1) * HEAD_DIM)
            s = lax.dot_general(q_ref[:, cs], k_ref[:, cs], (((1,), (1,)), ((), ())),
                                preferred_element_type=F32) + bias
            m_prev = m_ref[h]
            m_new = jnp.maximum(m_prev, jnp.max(s, axis=1, keepdims=True))
            alpha = jnp.exp2(m_prev - m_new)
            p = jnp.exp2(s - jnp.concatenate([m_new] * (s.shape[1] // LANES), axis=1))
            m_ref[h] = m_new
            v_aug = jnp.concatenate([v_ref[:, cs], jnp.ones((v_ref.shape[0], LANES), BF16)], axis=1)
            ca = slice(2 * h * HEAD_DIM, 2 * (h + 1) * HEAD_DIM)
            acc_ref[:, ca] = (jnp.concatenate([alpha, alpha], axis=1) * acc_ref[:, ca]
                              + jnp.dot(p.astype(BF16), v_aug, preferred_element_type=F32))

    @pl.when(j == last)
    def _():
        for h in range(N_HEADS_ATTN):
            cs = slice(h * HEAD_DIM, (h + 1) * HEAD_DIM)
            lo = 2 * h * HEAD_DIM
            o_ref[:, cs] = (acc_ref[:, lo:lo + HEAD_DIM]
                            / acc_ref[:, lo + HEAD_DIM:lo + 2 * HEAD_DIM]).astype(o_ref.dtype)


def _masked_attention(proj_qk, proj_v, mask, q_cb, k_cb, v_cb, seq):
    assert ATT_KT == IDX_KT and mask.shape == (seq // ATT_KT, seq, ATT_KT)
    tq, kt = ATT_TQ, ATT_KT
    kv_tile = lambda i, j: jnp.minimum((i + 1) * (tq // kt) - 1, j)
    return pl.pallas_call(
        _attn_kernel,
        grid=(seq // tq, seq // kt),
        in_specs=[pl.BlockSpec((tq, A_WIDTH), lambda i, j: (i, q_cb)),
                  pl.BlockSpec((kt, A_WIDTH), lambda i, j: (kv_tile(i, j), k_cb)),
                  pl.BlockSpec((kt, A_WIDTH), lambda i, j: (kv_tile(i, j), v_cb)),
                  pl.BlockSpec((1, tq, kt), lambda i, j: (kv_tile(i, j), i, 0))],
        out_specs=pl.BlockSpec((tq, A_WIDTH), lambda i, j: (i, 0)),
        out_shape=jax.ShapeDtypeStruct((seq, A_WIDTH), BF16),
        scratch_shapes=[pltpu.VMEM((N_HEADS_ATTN, tq, LANES), F32),
                        pltpu.VMEM((tq, 2 * A_WIDTH), F32)],
        compiler_params=_cparams(("parallel", "arbitrary"), 40),
        name="masked_attention",
    )(proj_qk, proj_qk, proj_v, mask)


def _retention_kernel(q_ref, k_ref, v_ref, g_ref, idec_ref, qdec_ref, kdec_ref, cdec_ref,
                      o_ref, state_ref):
    @pl.when(pl.program_id(0) == 0)
    def _():
        state_ref[...] = jnp.zeros(state_ref.shape, F32)

    for h in range(N_RET_HEADS):
        ck = slice(h * RET_QK_DIM, (h + 1) * RET_QK_DIM)
        cv = slice(h * RET_V_DIM, (h + 1) * RET_V_DIM)
        q = q_ref[:, ck]
        k = k_ref[:, ck]
        v = v_ref[:, cv]
        scores = lax.dot_general(q, k, (((1,), (1,)), ((), ())),
                                 preferred_element_type=F32) * idec_ref[h]
        inner = jnp.dot(scores.astype(BF16), v, preferred_element_type=F32)
        state = state_ref[h]
        cross = jnp.dot(q, state.astype(BF16), preferred_element_type=F32) * qdec_ref[h]
        y = inner + cross
        mu = jnp.mean(y, axis=-1, keepdims=True)
        var = jnp.mean(jnp.square(y - mu), axis=-1, keepdims=True)
        yn = (y - mu) * lax.rsqrt(var + EPS)
        g = g_ref[:, cv]
        o_ref[:, cv] = (g * jax.nn.sigmoid(g) * yn).astype(o_ref.dtype)
        k_dec_t = jnp.transpose(k.astype(F32) * kdec_ref[h]).astype(BF16)
        state_ref[h] = state * cdec_ref[h] + jnp.dot(k_dec_t, v, preferred_element_type=F32)


def _retention(proj_qk, proj_v, proj_g, q_cb, k_cb, v_cb, g_cb, seq):
    c = RET_CHUNK
    h = N_RET_HEADS
    log_gamma = jnp.log1p(-jnp.exp2(-5.0 - jnp.arange(h, dtype=F32)))
    i = jnp.arange(c, dtype=F32)
    diff = i[:, None] - i[None, :]
    idec = jnp.where(diff[None] >= 0,
                     jnp.exp(jnp.maximum(diff, 0.0)[None] * log_gamma[:, None, None]), 0.0)
    kdec = jnp.exp((c - 1 - i)[None, :] * log_gamma[:, None])
    qdec = jnp.exp((i + 1)[None, :] * log_gamma[:, None])
    cdec = jnp.exp(c * log_gamma)
    qdec_b = jnp.broadcast_to(qdec[:, :, None], (h, c, RET_V_DIM))
    kdec_b = jnp.broadcast_to(kdec[:, :, None], (h, c, RET_QK_DIM))
    cdec_b = jnp.broadcast_to(cdec[:, None, None], (h, RET_QK_DIM, RET_V_DIM))
    const = lambda shape: pl.BlockSpec(shape, lambda n: (0, 0, 0))
    return pl.pallas_call(
        _retention_kernel,
        grid=(seq // c,),
        in_specs=[pl.BlockSpec((c, RET_QK_WIDTH), lambda n: (n, q_cb)),
                  pl.BlockSpec((c, RET_QK_WIDTH), lambda n: (n, k_cb)),
                  pl.BlockSpec((c, RET_V_WIDTH), lambda n: (n, v_cb)),
                  pl.BlockSpec((c, RET_V_WIDTH), lambda n: (n, g_cb)),
                  const((h, c, c)), const((h, c, RET_V_DIM)), const((h, c, RET_QK_DIM)),
                  const((h, RET_QK_DIM, RET_V_DIM))],
        out_specs=pl.BlockSpec((c, RET_V_WIDTH), lambda n: (n, 0)),
        out_shape=jax.ShapeDtypeStruct((seq, RET_V_WIDTH), BF16),
        scratch_shapes=[pltpu.VMEM((h, RET_QK_DIM, RET_V_DIM), F32)],
        compiler_params=_cparams(("arbitrary",), 24),
        name="retention",
    )(proj_qk, proj_qk, proj_v, proj_g, idec, qdec_b, kdec_b, cdec_b)


def _mem_attn_kernel(q_ref, kv_ref, o_ref):
    for h in range(N_MEM_HEADS):
        cs = slice(h * MEM_HEAD_DIM, (h + 1) * MEM_HEAD_DIM)
        vs = slice(MEM_WIDTH + h * MEM_HEAD_DIM, MEM_WIDTH + (h + 1) * MEM_HEAD_DIM)
        s = lax.dot_general(q_ref[:, cs], kv_ref[:, cs], (((1,), (1,)), ((), ())),
                            preferred_element_type=F32)
        e = jnp.exp(s - jnp.max(s, axis=-1, keepdims=True))
        p = e / jnp.sum(e, axis=-1, keepdims=True)
        o_ref[:, cs] = jnp.dot(p.astype(BF16), kv_ref[:, vs],
                               preferred_element_type=F32).astype(o_ref.dtype)


def _mem_attention(proj_a, mem_kv, q_cb, seq, tm):
    n_mem = mem_kv.shape[0]
    return pl.pallas_call(
        _mem_attn_kernel,
        grid=(seq // tm,),
        in_specs=[pl.BlockSpec((tm, MEM_WIDTH), lambda i: (i, q_cb)),
                  pl.BlockSpec((n_mem, 2 * MEM_WIDTH), lambda i: (0, 0))],
        out_specs=pl.BlockSpec((tm, MEM_WIDTH), lambda i: (i, 0)),
        out_shape=jax.ShapeDtypeStruct((seq, MEM_WIDTH), BF16),
        compiler_params=_cparams(("parallel",), 24),
        name="memory_attention",
    )(proj_a, mem_kv)


def _merge_kernel(oa_ref, ob_ref, oc_ref, wa_ref, wb_ref, wc_ref, g0_ref, g1_ref, g2_ref, o_ref):
    for c in range(0, o_ref.shape[1], PROJ_CHUNK):
        cs = slice(c, c + PROJ_CHUNK)
        mixed = g0_ref[:, cs] * jnp.dot(oa_ref[...], wa_ref[:, cs], preferred_element_type=F32)
        mixed = mixed + g1_ref[:, cs] * jnp.dot(ob_ref[...], wb_ref[:, cs], preferred_element_type=F32)
        mixed = mixed + g2_ref[:, cs] * jnp.dot(oc_ref[...], wc_ref[:, cs], preferred_element_type=F32)
        o_ref[:, cs] = mixed.astype(o_ref.dtype)


def _merge(o_a, o_b, o_c, w_a, w_b, w_c, proj_b, gate_cb0, tm, tn):
    seq = o_a.shape[0]
    d = w_a.shape[1]
    per_gate = d // tn
    act = lambda w: pl.BlockSpec((tm, w), lambda j, i: (i, 0))
    wgt = lambda k: pl.BlockSpec((k, tn), lambda j, i: (0, j))
    gate = lambda g: pl.BlockSpec((tm, tn), lambda j, i: (i, gate_cb0 + g * per_gate + j))
    return pl.pallas_call(
        _merge_kernel,
        grid=(d // tn, seq // tm),
        in_specs=[act(o_a.shape[1]), act(o_b.shape[1]), act(o_c.shape[1]),
                  wgt(w_a.shape[0]), wgt(w_b.shape[0]), wgt(w_c.shape[0]),
                  gate(0), gate(1), gate(2)],
        out_specs=pl.BlockSpec((tm, tn), lambda j, i: (i, j)),
        out_shape=jax.ShapeDtypeStruct((seq, d), BF16),
        compiler_params=_cparams(("parallel", "arbitrary"), 48),
        name="gated_merge",
    )(o_a, o_b, o_c, w_a, w_b, w_c, proj_b, proj_b, proj_b)


def _mm_norm_res_kernel(a_ref, w_ref, x_ref, g_ref, *rest, with_next_norm):
    if with_next_norm:
        g2_ref, o_ref, h_ref = rest
    else:
        (o_ref,) = rest
    d = w_ref.shape[1]
    cols = [slice(c, c + PROJ_CHUNK) for c in range(0, d, PROJ_CHUNK)]
    accs = [jnp.dot(a_ref[...], w_ref[:, cs], preferred_element_type=F32) for cs in cols]
    ssq = sum(jnp.sum(acc * acc, axis=-1, keepdims=True) for acc in accs)
    inv = lax.rsqrt(ssq / d + EPS)
    x_new = [x_ref[:, cs] + acc * inv * g_ref[:, cs] for cs, acc in zip(cols, accs)]
    for cs, xc in zip(cols, x_new):
        o_ref[:, cs] = xc
    if with_next_norm:
        ssq2 = sum(jnp.sum(xc * xc, axis=-1, keepdims=True) for xc in x_new)
        inv2 = lax.rsqrt(ssq2 / d + EPS)
        for cs, xc in zip(cols, x_new):
            h_ref[:, cs] = (xc * inv2 * g2_ref[:, cs]).astype(h_ref.dtype)


def _mm_norm_res(a, w, x, g, g2, tm, vmem_mb, name):
    m, k = a.shape
    d = w.shape[1]
    row = lambda width: pl.BlockSpec((tm, width), lambda i: (i, 0))
    vec = pl.BlockSpec((1, d), lambda i: (0, 0))
    with_next_norm = g2 is not None
    in_specs = [row(k), pl.BlockSpec((k, d), lambda i: (0, 0), pipeline_mode=pl.Buffered(1)),
                row(d), vec]
    args = [a, w, x, g.reshape(1, d)]
    out_specs = [row(d)]
    out_shape = [jax.ShapeDtypeStruct((m, d), F32)]
    if with_next_norm:
        in_specs.append(vec)
        args.append(g2.reshape(1, d))
        out_specs.append(row(d))
        out_shape.append(jax.ShapeDtypeStruct((m, d), BF16))
    return pl.pallas_call(
        functools.partial(_mm_norm_res_kernel, with_next_norm=with_next_norm),
        grid=(m // tm,),
        in_specs=in_specs,
        out_specs=out_specs,
        out_shape=out_shape,
        compiler_params=_cparams(("parallel",), vmem_mb),
        name=name,
    )(*args)


def _swiglu_kernel(a_ref, wg_ref, wu_ref, o_ref, wg_bf_ref, wu_bf_ref):
    @pl.when(pl.program_id(1) == 0)
    def _():
        wg_bf_ref[...] = wg_ref[...].astype(BF16)
        wu_bf_ref[...] = wu_ref[...].astype(BF16)

    gate = jnp.dot(a_ref[...], wg_bf_ref[...], preferred_element_type=F32)
    up = jnp.dot(a_ref[...], wu_bf_ref[...], preferred_element_type=F32)
    o_ref[...] = (gate * jax.nn.sigmoid(gate) * up).astype(o_ref.dtype)


def _swiglu(a, w, tm, tn):
    m, k = a.shape
    d_ff = w.shape[1] // 2
    nt = d_ff // tn
    return pl.pallas_call(
        _swiglu_kernel,
        grid=(nt, m // tm),
        in_specs=[pl.BlockSpec((tm, k), lambda j, i: (i, 0)),
                  pl.BlockSpec((k, tn), lambda j, i: (0, j)),
                  pl.BlockSpec((k, tn), lambda j, i: (0, nt + j))],
        out_specs=pl.BlockSpec((tm, tn), lambda j, i: (i, j)),
        out_shape=jax.ShapeDtypeStruct((m, d_ff), BF16),
        scratch_shapes=[pltpu.VMEM((k, tn), BF16), pltpu.VMEM((k, tn), BF16)],
        compiler_params=_cparams(("arbitrary", "arbitrary"), 48),
        name="swiglu_in",
    )(a, w, w)


def _layer(x, mem, positions, g_pre_mix, g_mem, w_in, w_mem_kv, w_branch_a, w_branch_b,
           w_branch_c, w_out, g_post_mix, g_pre_ffn, w_ffn_in, w_ffn_out, g_post_ffn):
    seq, d = x.shape
    k_top = min(TOPK_MAX, seq // 4)

    sizes = (A_WIDTH, A_WIDTH, A_WIDTH, IDX_Q_WIDTH, IDX_DIM, N_IDX_HEADS,
             RET_QK_WIDTH, RET_QK_WIDTH, RET_V_WIDTH, RET_V_WIDTH, MEM_WIDTH, 3 * d)
    (o_aq, o_ak, o_av, o_iq, o_ik, o_iw, o_rq, o_rk, o_rv, o_rg, o_mq, o_gate) = (
        int(o) for o in np.cumsum((0,) + sizes[:-1]))
    w_in_t = jnp.swapaxes(w_in, 0, 1)

    tabs = _rope_tables(positions, 1024)
    (c128, s128, c64, sa64, sb64, ck, sak, sbk) = tabs
    t128 = (c128, s128, s128)
    t64 = (c64, sa64, sb64)

    h = _rmsnorm(x, g_pre_mix, 512)

    tn = 1024
    proj_iq = _proj(h, w_in_t, [o_iq, o_iq + tn // 2], t64, ROPE64, [IDX_DIM ** -0.5] * 2, BF16,
                    1024, tn, 48, "proj_index_q", pad_heads=True)
    proj_r = _proj(h, w_in_t, [o_aq, o_ak, o_rq, o_rk], t128, ROPE128,
                   [HEAD_DIM ** -0.5 * LOG2E, 1.0, 1.0, RET_QK_DIM ** -0.5], BF16, 1024, tn, 48,
                   "proj_rope")
    proj_p = _proj(h, w_in_t, [o_rv, o_rv + tn, o_av, o_mq], t128, PLAIN,
                   [1.0, 1.0, 1.0, MEM_HEAD_DIM ** -0.5], BF16, 1024, tn, 48, "proj_plain")
    gates = _proj(h, w_in_t, [o_gate + t * tn for t in range(3 * d // tn)], t128, SIGMOID,
                  [1.0] * (3 * d // tn), F32, 1024, tn, 52, "proj_gates")
    r_g = _proj(h, w_in_t, [o_rg, o_rg + tn], t128, PLAIN, [1.0] * 2, F32, 1024, tn, 52,
                "proj_ret_gate")
    assert o_iw == o_ik + IDX_DIM
    ikw = _proj(h, w_in_t, [o_ik], (ck, sak, sbk), ROPE64, [1.0], F32, 1024, LANES, 24,
                "proj_index_kw", valid_rows=IDX_DIM + N_IDX_HEADS)

    mask = _indexer_mask(ikw, proj_iq, 0, seq, k_top)
    o_a = _masked_attention(proj_r, proj_p, mask, 0, 1, 2, seq)

    o_b = _retention(proj_r, proj_p, r_g, 2, 3, 0, 0, seq)

    mem_n = _rmsnorm(mem, g_mem, mem.shape[0])
    mem_kv = _proj(mem_n, jnp.swapaxes(w_mem_kv, 0, 1), [0, tn], t128, PLAIN, [1.0] * 2, BF16,
                   mem.shape[0], tn, 40, "proj_mem_kv")
    o_c = _mem_attention(proj_p, mem_kv, 3, seq, 512)

    mixed = _merge(o_a, o_b, o_c, w_branch_a.astype(BF16), w_branch_b.astype(BF16),
                   w_branch_c.astype(BF16), gates, 0, 512, tn)
    x1, h2 = _mm_norm_res(mixed, w_out.astype(BF16), x, g_post_mix, g_pre_ffn, 512, 40,
                          "out_proj_norm_res")

    act = _swiglu(h2, w_ffn_in, 1024, 512)
    (x2,) = _mm_norm_res(act, w_ffn_out.astype(BF16), x1, g_post_ffn, None, 256, 48,
                         "ffn_out_norm_res")
    return x2


def kernel(x, mem, positions, g_pre_mix, g_mem, w_in, w_mem_kv, w_branch_a, w_branch_b, w_branch_c,
           w_out, g_post_mix, g_pre_ffn, w_ffn_in, w_ffn_out, g_post_ffn):
    assert x.shape[0] == 1 and g_pre_mix.shape[0] == 1
    out = _layer(x[0], mem[0], positions[0], g_pre_mix[0], g_mem[0], w_in[0], w_mem_kv[0],
                 w_branch_a[0], w_branch_b[0], w_branch_c[0], w_out[0], g_post_mix[0],
                 g_pre_ffn[0], w_ffn_in[0], w_ffn_out[0], g_post_ffn[0])
    return out[None]
```
